```python
import math
import jax, jax.numpy as jnp
from jax import lax
import numpy as np


D_MODEL = 1024
BATCH = 2
SEQ = 8192
DEPTH = 4
DEC_BATCH = 128
DEC_SEQ = 8
PAST_LEN = 2048
PAGE_SIZE = 128

ATTN_HEADS = 8
ATTN_KV_HEADS = 2
HEAD_DIM = 64
ATTN_REP = ATTN_HEADS // ATTN_KV_HEADS
ATTN_WIDTH = ATTN_HEADS * HEAD_DIM
KV_WIDTH = ATTN_KV_HEADS * HEAD_DIM
IDX_HEADS = 8
IDX_DIM = 64
IDX_TOPK_MAX = 256
Q_BLOCK = 128
GLA_HEADS = 4
GLA_DK = 64
GLA_DV = 128
GLA_K_WIDTH = GLA_HEADS * GLA_DK
GLA_V_WIDTH = GLA_HEADS * GLA_DV
GLA_GATE_RANK = 16
GLA_GATE_TAU = 16.0
GLA_CHUNK = 64
MIX_WIDTH = ATTN_WIDTH + GLA_V_WIDTH
D_FF = -(-8 * D_MODEL // (3 * 256)) * 256
REL_BUCKETS = 32
REL_MAX_DIST = 128
RMS_EPS = 1e-6
LN_EPS = 1e-5
PROJ_SPLIT = (ATTN_WIDTH, KV_WIDTH, KV_WIDTH, IDX_HEADS * IDX_DIM, IDX_DIM, IDX_HEADS,
              GLA_K_WIDTH, GLA_K_WIDTH, GLA_V_WIDTH, GLA_V_WIDTH, GLA_GATE_RANK)
IN_COLS = sum(PROJ_SPLIT)

kernel_name = 'hymba_dsa_gla_decoder_step'


def rmsnorm(x, g):
    x32 = x.astype(jnp.float32)
    y = x32 * lax.rsqrt(jnp.mean(x32 * x32, axis=-1, keepdims=True) + RMS_EPS)
    return (y * g.astype(jnp.float32)).astype(x.dtype)


def layernorm(x, g, b):
    x32 = x.astype(jnp.float32)
    xc = x32 - jnp.mean(x32, axis=-1, keepdims=True)
    y = xc * lax.rsqrt(jnp.mean(xc * xc, axis=-1, keepdims=True) + LN_EPS)
    return (y * g.astype(jnp.float32) + b.astype(jnp.float32)).astype(x.dtype)


def split_points():
    return tuple(int(c) for c in np.cumsum(PROJ_SPLIT)[:-1])


def rel_bucket(dist):
    n = jnp.maximum(dist, 0)
    max_exact = REL_BUCKETS // 2
    large = max_exact + (jnp.log(jnp.maximum(n, 1).astype(jnp.float32) / max_exact)
                         / math.log(REL_MAX_DIST / max_exact)
                         * (REL_BUCKETS - max_exact)).astype(jnp.int32)
    large = jnp.minimum(large, REL_BUCKETS - 1)
    return jnp.where(n < max_exact, n, large)


def project(h, w_in_l, kn_g, kn_b, gate_up_l, gate_b_l):
    B, T = h.shape[:2]
    z = jnp.einsum('btd,dc->btc', h, w_in_l)
    q, k, v, iq, ik, iw, gq, gk, gv, gr, glr = jnp.split(z, split_points(), axis=-1)
    q = q.reshape(B, T, ATTN_HEADS, HEAD_DIM)
    k = k.reshape(B, T, ATTN_KV_HEADS, HEAD_DIM)
    v = v.reshape(B, T, ATTN_KV_HEADS, HEAD_DIM)
    iq = iq.reshape(B, T, IDX_HEADS, IDX_DIM)
    ik = layernorm(ik, kn_g, kn_b)
    iw = iw * (IDX_HEADS ** -0.5 * IDX_DIM ** -0.5)
    gq = gq.reshape(B, T, GLA_HEADS, GLA_DK) * (GLA_DK ** -0.5)
    gk = gk.reshape(B, T, GLA_HEADS, GLA_DK)
    gv = gv.reshape(B, T, GLA_HEADS, GLA_DV)
    gate = jnp.einsum('btr,rc->btc', glr, gate_up_l) + gate_b_l
    log_a = (jax.nn.log_sigmoid(gate.astype(jnp.float32)) / GLA_GATE_TAU).reshape(B, T, GLA_HEADS, GLA_DK)
    return q, k, v, iq, ik, iw, gq, gk, gv, gr, log_a


def dsa_attention(q, iq, iw, t_pos, idx_keys, gather_kv, topk, rel_bias):
    B, T = q.shape[:2]
    L = idx_keys.shape[1]
    qb = math.gcd(T, Q_BLOCK)
    nb = T // qb
    key_pos = jnp.arange(L, dtype=jnp.int32)

    def blockify(a):
        return jnp.moveaxis(a.reshape((B, nb, qb) + a.shape[2:]), 1, 0)

    def one_block(args):
        qs, iqs, iws, tp = args
        dots = jnp.einsum('bqhd,bsd->bqhs', iqs, idx_keys)
        score = jnp.einsum('bqh,bqhs->bqs', iws, jax.nn.relu(dots)).astype(jnp.float32)
        admissible = key_pos[None, :] <= tp[:, None]
        score = jnp.where(admissible[None], score, -jnp.inf)
        _, sel = lax.top_k(score, topk)
        valid = sel <= tp[None, :, None]
        kg, vg = gather_kv(sel)
        qg = qs.reshape(B, qb, ATTN_KV_HEADS, ATTN_REP, HEAD_DIM)
        logits = jnp.einsum('bqgrd,bqkgd->bqgrk', qg, kg).astype(jnp.float32) * (HEAD_DIM ** -0.5)
        bias = rel_bias[rel_bucket(tp[None, :, None] - sel)]
        bias = jnp.moveaxis(bias, -1, 2).reshape(B, qb, ATTN_KV_HEADS, ATTN_REP, topk)
        logits = jnp.where(valid[:, :, None, None, :], logits + bias.astype(jnp.float32), -jnp.inf)
        p = jax.nn.softmax(logits, axis=-1).astype(vg.dtype)
        o = jnp.einsum('bqgrk,bqkgd->bqgrd', p, vg)
        return o.reshape(B, qb, ATTN_WIDTH)

    out = lax.map(one_block, (blockify(q), blockify(iq), blockify(iw), t_pos.reshape(nb, qb)))
    return jnp.moveaxis(out, 0, 1).reshape(B, T, ATTN_WIDTH)


def gla_chunked(q, k, v, log_a, s0):
    B, T, H, _ = q.shape
    C = min(GLA_CHUNK, T)
    pad = (-T) % C
    nc = (T + pad) // C

    def chunks(a):
        a = jnp.pad(a.astype(jnp.float32), ((0, 0), (0, pad), (0, 0), (0, 0)))
        return a.reshape(B, nc, C, H, a.shape[-1]).transpose(1, 0, 3, 2, 4)

    causal = jnp.tril(jnp.ones((C, C), dtype=bool))

    def step(S, xs):
        qc, kc, vc, gc = xs
        b = jnp.cumsum(gc, axis=2)
        o_inter = jnp.einsum('bhtd,bhde->bhte', qc * jnp.exp(b), S)
        diff = b[:, :, :, None, :] - b[:, :, None, :, :]
        decay = jnp.exp(jnp.where(causal[:, :, None], diff, -jnp.inf))
        A = jnp.einsum('bhtd,bhsd,bhtsd->bhts', qc, kc, decay)
        o_intra = jnp.einsum('bhts,bhse->bhte', A, vc)
        b_last = b[:, :, -1:, :]
        S_new = jnp.exp(b_last[:, :, 0, :])[..., None] * S + jnp.einsum('bhsd,bhse->bhde', kc * jnp.exp(b_last - b), vc)
        return S_new, o_inter + o_intra

    S_fin, o = lax.scan(step, s0.astype(jnp.float32), (chunks(q), chunks(k), chunks(v), chunks(log_a)))
    o = o.transpose(1, 0, 3, 2, 4).reshape(B, nc * C, H, GLA_DV)[:, :T]
    return o, S_fin


def gla_readout(o, r, g):
    B, T = o.shape[:2]
    y = o * lax.rsqrt(jnp.mean(o * o, axis=-1, keepdims=True) + RMS_EPS) * g.astype(jnp.float32)
    return y.reshape(B, T, GLA_V_WIDTH) * jax.nn.silu(r.astype(jnp.float32))


def residual_block(x, attn_o, gla_o, w_out_l, ffn_g, wg, wu, wd):
    mix = jnp.concatenate([attn_o.astype(x.dtype), gla_o.astype(x.dtype)], axis=-1)
    x = x + jnp.einsum('btc,cd->btd', mix, w_out_l)
    h = rmsnorm(x, ffn_g)
    u = jax.nn.silu(jnp.einsum('btd,df->btf', h, wg)) * jnp.einsum('btd,df->btf', h, wu)
    return x + jnp.einsum('btf,fd->btd', u, wd)


def setup_inputs(seed: int = 0) -> dict:
    key = jax.random.key(seed)
    ks = jax.random.split(key, 24)
    f32 = jnp.float32
    n_pages = PAST_LEN // PAGE_SIZE
    n_used = DEC_BATCH * n_pages
    n_pool = n_used + max(1, n_used // 4)

    def nrm(k, shape, s):
        return jax.random.normal(k, shape, f32) * s

    page_table = jax.random.permutation(ks[0], n_pool)[:n_used].reshape(DEC_BATCH, n_pages).astype(jnp.int32)
    return {
        'x_prompt': nrm(ks[1], (BATCH, SEQ, D_MODEL), 1.0),
        'x_sample': nrm(ks[2], (DEC_BATCH, DEC_SEQ, D_MODEL), 1.0),
        'cache_k': nrm(ks[3], (DEPTH, n_pool, PAGE_SIZE, ATTN_KV_HEADS, HEAD_DIM), 1.0),
        'cache_v': nrm(ks[4], (DEPTH, n_pool, PAGE_SIZE, ATTN_KV_HEADS, HEAD_DIM), 1.0),
        'cache_idx_k': nrm(ks[5], (DEPTH, n_pool, PAGE_SIZE, IDX_DIM), 1.0),
        'state_gla': nrm(ks[6], (DEPTH, DEC_BATCH, GLA_HEADS, GLA_DK, GLA_DV), 1.0),
        'page_table': page_table,
        'attn_norm_g': 1.0 + nrm(ks[7], (DEPTH, D_MODEL), 0.02),
        'w_in': nrm(ks[8], (DEPTH, D_MODEL, IN_COLS), D_MODEL ** -0.5),
        'idx_knorm_g': 1.0 + nrm(ks[9], (DEPTH, IDX_DIM), 0.02),
        'idx_knorm_b': nrm(ks[10], (DEPTH, IDX_DIM), 0.02),
        'gla_gate_up': nrm(ks[11], (DEPTH, GLA_GATE_RANK, GLA_K_WIDTH), GLA_GATE_RANK ** -0.5),
        'gla_gate_b': nrm(ks[12], (DEPTH, GLA_K_WIDTH), 0.1),
        'gla_onorm_g': 1.0 + nrm(ks[13], (DEPTH, GLA_DV), 0.02),
        'w_out': nrm(ks[14], (DEPTH, MIX_WIDTH, D_MODEL), MIX_WIDTH ** -0.5),
        'ffn_norm_g': 1.0 + nrm(ks[15], (DEPTH, D_MODEL), 0.02),
        'w_gate': nrm(ks[16], (DEPTH, D_MODEL, D_FF), D_MODEL ** -0.5),
        'w_up': nrm(ks[17], (DEPTH, D_MODEL, D_FF), D_MODEL ** -0.5),
        'w_down': nrm(ks[18], (DEPTH, D_FF, D_MODEL), D_FF ** -0.5),
        'rel_bias': nrm(ks[19], (REL_BUCKETS, ATTN_HEADS), 0.5),
        'final_norm_g': 1.0 + nrm(ks[20], (D_MODEL,), 0.02),
    }


def reference(x_prompt, x_sample, cache_k, cache_v, cache_idx_k, state_gla, page_table,
              attn_norm_g, w_in, idx_knorm_g, idx_knorm_b, gla_gate_up, gla_gate_b, gla_onorm_g,
              w_out, ffn_norm_g, w_gate, w_up, w_down, rel_bias, final_norm_g):
    B, T = x_prompt.shape[:2]
    DB, TS = x_sample.shape[:2]
    page = cache_k.shape[2]
    past_len = page_table.shape[1] * page
    topk_p = min(IDX_TOPK_MAX, T // 4)
    topk_s = min(IDX_TOPK_MAX, (past_len + TS) // 4)
    pos_p = jnp.arange(T, dtype=jnp.int32)
    pos_s = past_len + jnp.arange(TS, dtype=jnp.int32)
    seq_ids = jnp.arange(DB)[:, None, None]
    take = jax.vmap(lambda rows, i: rows[i])

    xp, xs = x_prompt, x_sample
    kp_l, vp_l, ikp_l, sp_l = [], [], [], []
    ks_l, vs_l, iks_l, ss_l = [], [], [], []
    for l in range(DEPTH):
        h = rmsnorm(xp, attn_norm_g[l])
        q, k, v, iq, ik, iw, gq, gk, gv, gr, la = project(h, w_in[l], idx_knorm_g[l], idx_knorm_b[l], gla_gate_up[l], gla_gate_b[l])

        def gather_p(sel, k=k, v=v):
            return take(k, sel), take(v, sel)

        a_o = dsa_attention(q, iq, iw, pos_p, ik, gather_p, topk_p, rel_bias)
        g_o, S = gla_chunked(gq, gk, gv, la, jnp.zeros((B, GLA_HEADS, GLA_DK, GLA_DV), jnp.float32))
        g_o = gla_readout(g_o, gr, gla_onorm_g[l])
        xp = residual_block(xp, a_o, g_o, w_out[l], ffn_norm_g[l], w_gate[l], w_up[l], w_down[l])
        kp_l.append(k); vp_l.append(v); ikp_l.append(ik); sp_l.append(S.astype(state_gla.dtype))

        h = rmsnorm(xs, attn_norm_g[l])
        q, k, v, iq, ik, iw, gq, gk, gv, gr, la = project(h, w_in[l], idx_knorm_g[l], idx_knorm_b[l], gla_gate_up[l], gla_gate_b[l])
        ck, cv = cache_k[l], cache_v[l]
        past_ik = cache_idx_k[l][page_table].reshape(DB, past_len, IDX_DIM)
        keys_idx = jnp.concatenate([past_ik, ik.astype(past_ik.dtype)], axis=1)

        def gather_s(sel, ck=ck, cv=cv, k=k, v=v):
            is_new = (sel >= past_len)[..., None, None]
            sp = jnp.minimum(sel, past_len - 1)
            phys = page_table[seq_ids, sp // page]
            off = sp % page
            sn = jnp.clip(sel - past_len, 0, TS - 1)
            kg = jnp.where(is_new, take(k, sn).astype(ck.dtype), ck[phys, off])
            vg = jnp.where(is_new, take(v, sn).astype(cv.dtype), cv[phys, off])
            return kg, vg

        a_o = dsa_attention(q, iq, iw, pos_s, keys_idx, gather_s, topk_s, rel_bias)
        g_o, S = gla_chunked(gq, gk, gv, la, state_gla[l])
        g_o = gla_readout(g_o, gr, gla_onorm_g[l])
        xs = residual_block(xs, a_o, g_o, w_out[l], ffn_norm_g[l], w_gate[l], w_up[l], w_down[l])
        ks_l.append(k); vs_l.append(v); iks_l.append(ik); ss_l.append(S.astype(state_gla.dtype))

    y_prompt = rmsnorm(xp, final_norm_g)
    y_sample = rmsnorm(xs, final_norm_g)
    new_k_prompt = jnp.stack(kp_l)
    new_v_prompt = jnp.stack(vp_l)
    new_idx_k_prompt = jnp.stack(ikp_l)
    new_gla_prompt = jnp.stack(sp_l)
    new_k_sample = jnp.stack(ks_l)
    new_v_sample = jnp.stack(vs_l)
    new_idx_k_sample = jnp.stack(iks_l)
    new_gla_sample = jnp.stack(ss_l)
    return (y_prompt, y_sample, new_k_prompt, new_v_prompt, new_idx_k_prompt, new_gla_prompt,
            new_k_sample, new_v_sample, new_idx_k_sample, new_gla_sample)
```

```python
import functools
import math

import jax
import jax.numpy as jnp
import numpy as np
from jax import lax
from jax.experimental import pallas as pl
from jax.experimental.pallas import tpu as pltpu

F32 = jnp.float32
BF16 = jnp.bfloat16
I32 = jnp.int32

ATTN_HEADS = 8
ATTN_KV_HEADS = 2
HEAD_DIM = 64
ATTN_WIDTH = ATTN_HEADS * HEAD_DIM
KV_WIDTH = ATTN_KV_HEADS * HEAD_DIM
IDX_HEADS = 8
IDX_DIM = 64
IDX_TOPK_MAX = 256
GLA_HEADS = 4
GLA_DK = 64
GLA_DV = 128
GLA_K_WIDTH = GLA_HEADS * GLA_DK
GLA_V_WIDTH = GLA_HEADS * GLA_DV
GLA_GATE_RANK = 16
GLA_GATE_TAU = 16.0
GLA_CHUNK = 64
GLA_SUB = 16
REL_BUCKETS = 32
REL_MAX_DIST = 128
RMS_EPS = 1e-6
LN_EPS = 1e-5

LANES = 128
SUBLANES = 8
VMEM_LIMIT = 56 * 1024 * 1024

TOKEN_TILE = 512
FFN_CHUNK = 256
ATTN_TILE = 256
GLA_STEP = 512

INT_MIN = np.int32(-2 ** 31)
INT_MAX = np.int32(2 ** 31 - 1)
NEG = -1e30

C_Q = 0
C_K = 512
C_V = 640
C_IQ = 768
C_IK = 1280
C_IW = 1408
C_GQ = 1536
C_GK = 1792
C_GV = 2048
C_GR = 2560
C_GLR = 3072
C_END = 3200


def _nt(a, b):
    return lax.dot_general(a, b, (((1,), (1,)), ((), ())), preferred_element_type=F32)


def _tn(a, b, precision=None):
    return lax.dot_general(a, b, (((0,), (0,)), ((), ())), preferred_element_type=F32,
                           precision=precision)


def _rep(x, n):
    return x if n == 1 else jnp.concatenate([x] * n, axis=1)


def _const_spec(shape):
    nd = len(shape)
    return pl.BlockSpec(shape, lambda *_: (0,) * nd, pipeline_mode=pl.Buffered(1))


def _params(sem):
    return pltpu.CompilerParams(dimension_semantics=sem, vmem_limit_bytes=VMEM_LIMIT)


def _inproj_kernel(x_ref, g_ref, w_ref, lng_ref, lnb_ref, gup_ref, gb_ref,
                   q_ref, k_ref, v_ref, kb_ref, vb_ref, iq_ref, ik_ref, ik2_ref, iw_ref,
                   gq_ref, gk_ref, gv_ref, gr_ref, la_ref):
    x = x_ref[...]
    ms = jnp.mean(x * x, axis=-1, keepdims=True)
    h = (x * lax.rsqrt(ms + RMS_EPS) * g_ref[...]).astype(BF16)

    def mm(a, b):
        return jnp.dot(h, w_ref[:, a:b], preferred_element_type=F32)

    q_ref[...] = mm(C_Q, C_K).astype(BF16)
    k = mm(C_K, C_V)
    k_ref[...] = k
    kb_ref[...] = k.astype(BF16)
    v = mm(C_V, C_IQ)
    v_ref[...] = v
    vb_ref[...] = v.astype(BF16)
    iq_ref[...] = mm(C_IQ, C_IK).astype(BF16)

    z = mm(C_IK, C_IW)
    mu = jnp.mean(z, axis=-1, keepdims=True)
    zc = z - mu
    var = jnp.mean(zc * zc, axis=-1, keepdims=True)
    ikn = zc * lax.rsqrt(var + LN_EPS) * lng_ref[...] + lnb_ref[...]
    ik_ref[...] = ikn[:, :IDX_DIM]
    ik2_ref[...] = ikn.astype(BF16)

    iw_ref[...] = mm(C_IW, C_GQ) * (IDX_HEADS ** -0.5 * IDX_DIM ** -0.5)
    gq_ref[...] = mm(C_GQ, C_GK) * (GLA_DK ** -0.5)
    gk_ref[...] = mm(C_GK, C_GV)
    gv_ref[...] = mm(C_GV, C_GR)
    gr_ref[...] = mm(C_GR, C_GLR)
    glr = mm(C_GLR, C_END).astype(BF16)
    gate = jnp.dot(glr, gup_ref[...], preferred_element_type=F32) + gb_ref[...]
    log_sig = jnp.minimum(gate, 0.0) - jnp.log1p(jnp.exp(-jnp.abs(gate)))
    la_ref[...] = log_sig / GLA_GATE_TAU


def _inproj(x, g, w, lng, lnb, gup, gb):
    n, d = x.shape
    tm = TOKEN_TILE
    row = lambda c: pl.BlockSpec((tm, c), lambda i: (i, 0))
    outs = [(ATTN_WIDTH, BF16), (KV_WIDTH, F32), (KV_WIDTH, F32), (KV_WIDTH, BF16), (KV_WIDTH, BF16),
            (IDX_HEADS * IDX_DIM, BF16), (IDX_DIM, F32), (2 * IDX_DIM, BF16), (LANES, F32),
            (GLA_K_WIDTH, F32), (GLA_K_WIDTH, F32), (GLA_V_WIDTH, F32), (GLA_V_WIDTH, F32),
            (GLA_K_WIDTH, F32)]
    return pl.pallas_call(
        _inproj_kernel,
        grid=(n // tm,),
        in_specs=[row(d), _const_spec((1, d)), _const_spec(w.shape), _const_spec(lng.shape),
                  _const_spec(lnb.shape), _const_spec(gup.shape), _const_spec(gb.shape)],
        out_specs=[row(c) for c, _ in outs],
        out_shape=[jax.ShapeDtypeStruct((n, c), dt) for c, dt in outs],
        compiler_params=_params(("parallel",)),
        name="inproj",
    )(x, g, w, lng, lnb, gup, gb)


def _ffn_kernel(x_ref, ao_ref, go_ref, wo_ref, g_ref, wg_ref, wu_ref, wd_ref, o_ref):
    half = ao_ref.shape[1]
    x1 = (x_ref[...]
          + jnp.dot(ao_ref[...], wo_ref[:half, :], preferred_element_type=F32)
          + jnp.dot(go_ref[...], wo_ref[half:, :], preferred_element_type=F32))
    ms = jnp.mean(x1 * x1, axis=-1, keepdims=True)
    h = (x1 * lax.rsqrt(ms + RMS_EPS) * g_ref[...]).astype(BF16)
    acc = None
    d_ff = wg_ref.shape[1]
    for c in range(0, d_ff, FFN_CHUNK):
        gt = jnp.dot(h, wg_ref[:, c:c + FFN_CHUNK], preferred_element_type=F32)
        up = jnp.dot(h, wu_ref[:, c:c + FFN_CHUNK], preferred_element_type=F32)
        u = (gt * jax.nn.sigmoid(gt) * up).astype(BF16)
        part = jnp.dot(u, wd_ref[c:c + FFN_CHUNK, :], preferred_element_type=F32)
        acc = part if acc is None else acc + part
    o_ref[...] = x1 + acc


def _ffn(x, ao, go, wo, g, wg, wu, wd):
    n, d = x.shape
    tm = TOKEN_TILE
    assert wg.shape[1] % FFN_CHUNK == 0
    row = lambda c: pl.BlockSpec((tm, c), lambda i: (i, 0))
    return pl.pallas_call(
        _ffn_kernel,
        grid=(n // tm,),
        in_specs=[row(d), row(ao.shape[1]), row(go.shape[1]), _const_spec(wo.shape),
                  _const_spec((1, d)), _const_spec(wg.shape), _const_spec(wu.shape),
                  _const_spec(wd.shape)],
        out_specs=row(d),
        out_shape=jax.ShapeDtypeStruct((n, d), F32),
        compiler_params=_params(("parallel",)),
        name="outproj_ffn",
    )(x, ao, go, wo, g, wg, wu, wd)


def _rmsnorm_kernel(x_ref, g_ref, o_ref):
    x = x_ref[...]
    ms = jnp.mean(x * x, axis=-1, keepdims=True)
    o_ref[...] = x * lax.rsqrt(ms + RMS_EPS) * g_ref[...]


def _rmsnorm(x, g):
    n, d = x.shape
    tm = TOKEN_TILE
    return pl.pallas_call(
        _rmsnorm_kernel,
        grid=(n // tm,),
        in_specs=[pl.BlockSpec((tm, d), lambda i: (i, 0)), _const_spec((1, d))],
        out_specs=pl.BlockSpec((tm, d), lambda i: (i, 0)),
        out_shape=jax.ShapeDtypeStruct((n, d), F32),
        compiler_params=_params(("parallel",)),
        name="final_norm",
    )(x, g)


def _rel_bucket(dist):
    n = jnp.maximum(dist, 0)
    max_exact = REL_BUCKETS // 2
    large = max_exact + (jnp.log(jnp.maximum(n, 1).astype(F32) / max_exact)
                         / math.log(REL_MAX_DIST / max_exact)
                         * (REL_BUCKETS - max_exact)).astype(I32)
    large = jnp.minimum(large, REL_BUCKETS - 1)
    return jnp.where(n < max_exact, n, large)


def _bias_kernel(rb_ref, o_ref, *, offsets):
    _, nh, r, c = o_ref.shape
    row = lax.broadcasted_iota(I32, (r, c), 0)
    col = lax.broadcasted_iota(I32, (r, c), 1)
    for t, off in enumerate(offsets):
        bucket = _rel_bucket(row - col + off)
        for h in range(nh):
            val = jnp.zeros((r, c), F32)
            for b in range(REL_BUCKETS):
                val = jnp.where(bucket == b, rb_ref[b, h], val)
            o_ref[t, h] = val


def _bias_tables(rel_bias, rows, cols, offsets):
    nh = rel_bias.shape[1]
    return pl.pallas_call(
        functools.partial(_bias_kernel, offsets=tuple(offsets)),
        in_specs=[pl.BlockSpec(memory_space=pltpu.SMEM)],
        out_shape=jax.ShapeDtypeStruct((len(offsets), nh, rows, cols), F32),
        name="bias_tables",
    )(rel_bias)


def _score_key(s):
    bits = lax.bitcast_convert_type(s, I32)
    return bits ^ ((bits >> 31) & INT_MAX)


def _topk_threshold(count_ge, rows, topk):
    def bit_body(it, u):
        bit = lax.shift_left(jnp.int32(1), 31 - it)
        u2 = u | bit
        cnt = count_ge(u2 ^ INT_MIN)
        return jnp.where(cnt >= topk, u2, u)

    u = lax.fori_loop(0, 32, bit_body, jnp.zeros((rows, LANES), I32))
    return u ^ INT_MIN


def _tie_limit(count_eq_below, need, rows, idx_bits):
    def bit_body(it, p):
        bit = lax.shift_left(jnp.int32(1), idx_bits - 1 - it)
        p2 = p | bit
        cnt = count_eq_below(p2)
        return jnp.where(cnt < need, p2, p)

    return lax.fori_loop(0, idx_bits, bit_body, jnp.zeros((rows, LANES), I32))


def _dsa_prompt_kernel(iq_ref, iw_ref, q_ref, ik2_ref, k_ref, v_ref, btab_ref, cfar_ref,
                       o_ref,
                       skey_ref, iqm_ref, qm_ref, iwr_ref, m_ref, l_ref, acc_ref, p_ref,
                       *, tq, topk, idx_bits):
    i = pl.program_id(1)
    nrep = tq // LANES
    lane = lax.broadcasted_iota(I32, (1, LANES), 1)
    lo_half = lane < HEAD_DIM
    row_id = lax.broadcasted_iota(I32, (tq, tq), 0)
    col_id = lax.broadcasted_iota(I32, (tq, tq), 1)

    for h in range(IDX_HEADS):
        grp = iq_ref[:, LANES * (h // 2):LANES * (h // 2 + 1)]
        iqm_ref[h] = jnp.where(lo_half if h % 2 == 0 else ~lo_half, grp, jnp.zeros_like(grp))
        iwr_ref[h] = jnp.broadcast_to(iw_ref[:, h:h + 1], (tq, LANES))
    for h in range(ATTN_HEADS):
        p, g = h % 4, h // 4
        grp = q_ref[:, LANES * p:LANES * (p + 1)]
        qm_ref[h] = jnp.where(lo_half if g == 0 else ~lo_half, grp, jnp.zeros_like(grp))
        m_ref[h] = jnp.full((tq, LANES), NEG, F32)
        l_ref[h] = jnp.zeros((tq, LANES), F32)
        acc_ref[h] = jnp.zeros((tq, LANES), F32)

    def key_rows(j):
        return pl.ds(pl.multiple_of(j * tq, tq), tq)

    def score_tile(j, diag):
        ikt = ik2_ref[key_rows(j), :]
        s = jnp.zeros((tq, tq), F32)
        for h in range(IDX_HEADS):
            d = _nt(iqm_ref[h], ikt)
            s = s + _rep(iwr_ref[h], nrep) * jnp.maximum(d, 0.0)
        key = _score_key(s)
        if diag:
            key = jnp.where(col_id <= row_id, key, INT_MIN)
        skey_ref[j] = key

    def score_body(j, c):
        score_tile(j, False)
        return c

    lax.fori_loop(0, i, score_body, 0)
    score_tile(i, True)

    def count_tiles(pred):
        def body(j, acc):
            kt = skey_ref[j]
            for c in range(nrep):
                acc = acc + jnp.where(pred(kt[:, c * LANES:(c + 1) * LANES], j * tq + c * LANES),
                                      1.0, 0.0)
            return acc
        acc = lax.fori_loop(0, i + 1, body, jnp.zeros((tq, LANES), F32))
        return jnp.sum(acc, axis=1, keepdims=True)

    thr = _topk_threshold(lambda cand: count_tiles(lambda kt, _: kt >= cand), tq, topk)

    cnt_gt = count_tiles(lambda kt, _: kt > thr)
    cnt_ge = count_tiles(lambda kt, _: kt >= thr)
    need = topk - cnt_gt
    real_thr = thr[:, :1] > INT_MIN
    tie = jnp.logical_and(cnt_ge > topk, real_thr)
    p_ref[...] = jnp.broadcast_to(jnp.where(real_thr, INT_MAX, jnp.int32(-1)), (tq, LANES))

    @pl.when(jnp.max(jnp.where(tie, 1.0, 0.0)) > 0.0)
    def _():
        def count_eq_below(p):
            return count_tiles(lambda kt, base: jnp.logical_and(kt == thr, base + lane < p))
        plim = _tie_limit(count_eq_below, need, tq, idx_bits)
        p_ref[...] = jnp.where(tie, plim, p_ref[...])

    plim = p_ref[...]
    thr_full = _rep(thr, nrep)

    def attn_tile(j, bias_of):
        kt = skey_ref[j]
        col_lim = _rep(plim - j * tq, nrep)
        sel = jnp.logical_or(kt > thr_full,
                             jnp.logical_and(kt == thr_full, col_id <= col_lim))
        kb = k_ref[key_rows(j), :]
        vb = v_ref[key_rows(j), :]
        for h in range(ATTN_HEADS):
            lg = _nt(qm_ref[h], kb) * (HEAD_DIM ** -0.5) + bias_of(h)
            lg = jnp.where(sel, lg, NEG)
            m_old = m_ref[h]
            m_new = jnp.maximum(m_old, jnp.max(lg, axis=1, keepdims=True))
            alpha = jnp.exp(m_old - m_new)
            pexp = jnp.exp(lg - _rep(m_new, nrep))
            l_ref[h] = alpha * l_ref[h] + jnp.sum(pexp, axis=1, keepdims=True)
            acc_ref[h] = alpha * acc_ref[h] + jnp.dot(pexp.astype(BF16), vb,
                                                      preferred_element_type=F32)
            m_ref[h] = m_new

    def far_body(j, c):
        attn_tile(j, lambda h: cfar_ref[h])
        return c

    lax.fori_loop(0, i - 1, far_body, 0)

    @pl.when(i >= 1)
    def _():
        attn_tile(i - 1, lambda h: btab_ref[0, h])

    attn_tile(i, lambda h: btab_ref[1, h])

    for p in range(4):
        o_lo = acc_ref[p] / l_ref[p]
        o_hi = acc_ref[p + 4] / l_ref[p + 4]
        o_ref[:, LANES * p:LANES * (p + 1)] = jnp.where(lo_half, o_lo, o_hi).astype(o_ref.dtype)


def _dsa_prompt(iq, iw, q, ik2, kb, vb, btab, cfar, batch, seq, topk):
    tq = min(ATTN_TILE, seq)
    assert seq % tq == 0 and tq % LANES == 0
    nq = seq // tq
    idx_bits = max(1, int(seq).bit_length())
    qrow = lambda c: pl.BlockSpec((tq, c), lambda b, i: (b * nq + i, 0))
    seqblk = lambda c: pl.BlockSpec((seq, c), lambda b, i: (b, 0))
    kern = functools.partial(_dsa_prompt_kernel, tq=tq, topk=float(topk), idx_bits=idx_bits)
    return pl.pallas_call(
        kern,
        grid=(batch, nq),
        in_specs=[qrow(IDX_HEADS * IDX_DIM), qrow(LANES), qrow(ATTN_WIDTH),
                  seqblk(2 * IDX_DIM), seqblk(KV_WIDTH), seqblk(KV_WIDTH),
                  _const_spec(btab.shape), pl.BlockSpec(memory_space=pltpu.SMEM)],
        out_specs=qrow(ATTN_WIDTH),
        out_shape=jax.ShapeDtypeStruct((batch * seq, ATTN_WIDTH), BF16),
        scratch_shapes=[pltpu.VMEM((nq, tq, tq), I32),
                        pltpu.VMEM((IDX_HEADS, tq, LANES), BF16),
                        pltpu.VMEM((ATTN_HEADS, tq, LANES), BF16),
                        pltpu.VMEM((IDX_HEADS, tq, LANES), F32),
                        pltpu.VMEM((ATTN_HEADS, tq, LANES), F32),
                        pltpu.VMEM((ATTN_HEADS, tq, LANES), F32),
                        pltpu.VMEM((ATTN_HEADS, tq, LANES), F32),
                        pltpu.VMEM((tq, LANES), I32)],
        compiler_params=_params(("parallel", "arbitrary")),
        name="dsa_prompt",
    )(iq, iw, q, ik2, kb, vb, btab, cfar)


def _dsa_sample_kernel(pt_ref, iq_ref, iwr_ref, q_ref, ikn_ref, kn_ref, vn_ref, bias_ref, *rest,
                       n_pages, page, ts, topk, idx_bits):
    ikp = rest[:n_pages]
    kp = rest[n_pages:2 * n_pages]
    vp = rest[2 * n_pages:3 * n_pages]
    o_ref = rest[3 * n_pages]
    skey_ref, lg_ref = rest[3 * n_pages + 1:]
    del pt_ref
    nt = n_pages + 1
    lane = lax.broadcasted_iota(I32, (1, LANES), 1)
    lo_half = lane < HEAD_DIM
    iq = iq_ref[...]
    iwr = iwr_ref[...]
    qh = q_ref[...]

    def score(keys_bf):
        d = _nt(iq, keys_bf)
        w = iwr * jnp.maximum(d, 0.0)
        s = w[0:ts]
        for h in range(1, IDX_HEADS):
            s = s + w[h * ts:(h + 1) * ts]
        return _score_key(s)

    for pg in range(n_pages):
        skey_ref[:, pg * page:(pg + 1) * page] = score(ikp[pg][...].astype(BF16))
    new_key = score(ikn_ref[...].astype(BF16))
    qrow = lax.broadcasted_iota(I32, (ts, page), 0)
    qcol = lax.broadcasted_iota(I32, (ts, page), 1)
    skey_ref[:, n_pages * page:] = jnp.where(qcol <= qrow, new_key, INT_MIN)

    def count_tiles(pred):
        acc = jnp.zeros((ts, LANES), F32)
        for c in range(nt):
            acc = acc + jnp.where(pred(skey_ref[:, c * LANES:(c + 1) * LANES], c * LANES), 1.0, 0.0)
        return jnp.sum(acc, axis=1, keepdims=True)

    thr = _topk_threshold(lambda cand: count_tiles(lambda kt, _: kt >= cand), ts, topk)
    cnt_gt = count_tiles(lambda kt, _: kt > thr)
    cnt_ge = count_tiles(lambda kt, _: kt >= thr)
    need = topk - cnt_gt
    real_thr = thr[:, :1] > INT_MIN
    tie = jnp.logical_and(cnt_ge > topk, real_thr)

    def count_eq_below(p):
        return count_tiles(lambda kt, base: jnp.logical_and(kt == thr, base + lane < p))

    plim = _tie_limit(count_eq_below, need, ts, idx_bits)
    plim = jnp.where(tie, plim,
                     jnp.broadcast_to(jnp.where(real_thr, INT_MAX, jnp.int32(-1)), (ts, LANES)))

    ktot = nt * page
    for pg in range(n_pages):
        lg_ref[:, pg * page:(pg + 1) * page] = _nt(qh, kp[pg][...].astype(BF16))
    lg_ref[:, n_pages * page:] = _nt(qh, kn_ref[...])
    kt = skey_ref[...]
    thr_full = _rep(thr, ktot // LANES)
    idx = lax.broadcasted_iota(I32, (ts, ktot), 1)
    sel = jnp.logical_or(kt > thr_full,
                         jnp.logical_and(kt == thr_full, idx <= _rep(plim, ktot // LANES)))
    sel = jnp.concatenate([sel.astype(F32)] * ATTN_HEADS, axis=0) > 0.5
    lg = lg_ref[...] * (HEAD_DIM ** -0.5) + bias_ref[...]
    lg = jnp.where(sel, lg, NEG)
    m = jnp.max(lg, axis=1, keepdims=True)
    pexp = jnp.exp(lg - m)
    denom = jnp.sum(pexp, axis=1, keepdims=True)
    pb = pexp.astype(BF16)
    acc = jnp.dot(pb[:, n_pages * page:], vn_ref[...], preferred_element_type=F32)
    for pg in range(n_pages):
        acc = acc + jnp.dot(pb[:, pg * page:(pg + 1) * page], vp[pg][...].astype(BF16),
                            preferred_element_type=F32)
    o = acc / denom
    half = (ATTN_HEADS // 2) * ts
    o_ref[...] = jnp.where(lo_half, o[:half], o[half:]).astype(o_ref.dtype)


def _dsa_sample(page_table, iq, iwr, qh, ikn, kn, vn, bias, cidx, ck, cv, layer, topk):
    nseq, n_pages = page_table.shape
    page = cidx.shape[2]
    ts = iq.shape[1] // IDX_HEADS
    assert page == LANES
    idx_bits = int((n_pages + 1) * page).bit_length()
    seq3 = lambda a: pl.BlockSpec((None,) + a.shape[1:], lambda b, pt: (b, 0, 0))

    def page_spec(a, pg):
        return pl.BlockSpec((None, None) + a.shape[2:], lambda b, pt: (layer, pt[b, pg], 0, 0))

    kern = functools.partial(_dsa_sample_kernel, n_pages=n_pages, page=page, ts=ts,
                             topk=float(topk), idx_bits=idx_bits)
    ktot = (n_pages + 1) * page
    in_specs = ([seq3(iq), seq3(iwr), seq3(qh), seq3(ikn), seq3(kn), seq3(vn),
                 pl.BlockSpec(bias.shape, lambda b, pt: (0, 0))]
                + [page_spec(cidx, pg) for pg in range(n_pages)]
                + [page_spec(ck, pg) for pg in range(n_pages)]
                + [page_spec(cv, pg) for pg in range(n_pages)])
    rows_out = (ATTN_HEADS // 2) * ts
    return pl.pallas_call(
        kern,
        grid_spec=pltpu.PrefetchScalarGridSpec(
            num_scalar_prefetch=1,
            grid=(nseq,),
            in_specs=in_specs,
            out_specs=pl.BlockSpec((None, rows_out, LANES), lambda b, pt: (b, 0, 0)),
            scratch_shapes=[pltpu.VMEM((ts, ktot), I32),
                            pltpu.VMEM((ATTN_HEADS * ts, ktot), F32)]),
        out_shape=jax.ShapeDtypeStruct((nseq, rows_out, LANES), BF16),
        compiler_params=_params(("arbitrary",)),
        name="dsa_sample",
    )(page_table, iq, iwr, qh, ikn, kn, vn, bias,
      *([cidx] * n_pages), *([ck] * n_pages), *([cv] * n_pages))


def _gla_chunk(q_ref, k_ref, v_ref, la_ref, r_ref, g_ref, o_ref, s_ref, b_scr, o_scr,
               row0, chunk, sub):
    kw = GLA_K_WIDTH
    rows = pl.ds(row0, chunk)
    q = q_ref[rows, :]
    k = k_ref[rows, :]
    v = v_ref[rows, :]
    la = la_ref[rows, :]
    hi = lax.Precision.HIGHEST
    tri = (lax.broadcasted_iota(I32, (chunk, chunk), 0)
           >= lax.broadcasted_iota(I32, (chunk, chunk), 1)).astype(F32)
    b = jnp.dot(tri, la, preferred_element_type=F32, precision=hi)
    b_scr[...] = b
    lane_head = lax.broadcasted_iota(I32, (1, kw), 1) // GLA_DK
    row_head = lax.broadcasted_iota(I32, (kw, GLA_DV), 0) // GLA_DK
    s_old = s_ref[...]
    s_bf = s_old.astype(BF16)

    qe = q * jnp.exp(b)
    b_last_col = _tn(la, jnp.ones((chunk, GLA_DV), F32), precision=hi)
    b_last = b[chunk - 1:chunk, :]
    ke = (k * jnp.exp(b_last - b)).astype(BF16)
    s_new = jnp.exp(b_last_col) * s_old
    for h in range(GLA_HEADS):
        qh = jnp.where(lane_head == h, qe, 0.0).astype(BF16)
        o_scr[:, h * GLA_DV:(h + 1) * GLA_DV] = jnp.dot(qh, s_bf, preferred_element_type=F32)
        kv = _tn(ke, v[:, h * GLA_DV:(h + 1) * GLA_DV].astype(BF16))
        s_new = s_new + jnp.where(row_head == h, kv, 0.0)
    s_ref[...] = s_new

    nsub = chunk // sub
    seg = (lax.broadcasted_iota(I32, (kw, GLA_V_WIDTH), 0) // GLA_DK
           == lax.broadcasted_iota(I32, (kw, GLA_V_WIDTH), 1) // GLA_DV).astype(BF16)
    t_id = lax.broadcasted_iota(I32, (sub, kw), 0)
    v_bf = v.astype(BF16)
    col = lax.broadcasted_iota(I32, (sub, chunk), 1)

    def sub_body(i, c):
        r0 = pl.multiple_of(i * sub, sub)
        srows = pl.ds(row0 + r0, sub)
        qb = q_ref[srows, :]
        kb = k_ref[srows, :]
        vb = v_ref[srows, :]
        bb = b_scr[pl.ds(r0, sub), :]
        base = bb[0:1, :] - la_ref[pl.ds(row0 + r0, 1), :]
        o_blk = o_scr[pl.ds(r0, sub), :]
        o_h = [o_blk[:, h * GLA_DV:(h + 1) * GLA_DV] for h in range(GLA_HEADS)]
        if nsub > 1:
            qi = qb * jnp.exp(bb - base)
            kpre = (k * jnp.exp(jnp.minimum(base - b, 0.0))).astype(BF16)
            for h in range(GLA_HEADS):
                qih = jnp.where(lane_head == h, qi, 0.0).astype(BF16)
                a = jnp.where(col < r0, _nt(qih, kpre), 0.0).astype(BF16)
                o_h[h] = o_h[h] + jnp.dot(a, v_bf[:, h * GLA_DV:(h + 1) * GLA_DV],
                                          preferred_element_type=F32)
        o_blk = jnp.concatenate(o_h, axis=1)
        prods = []
        for s in range(sub):
            e = jnp.exp(jnp.minimum(bb - bb[s:s + 1, :], 0.0))
            prods.append(jnp.where(t_id >= s, qb * kb[s:s + 1, :] * e, 0.0))
        pall = jnp.concatenate(prods, axis=0).astype(BF16)
        abc = jnp.dot(pall, seg, preferred_element_type=F32)
        for s in range(sub):
            o_blk = o_blk + abc[s * sub:(s + 1) * sub, :] * vb[s:s + 1, :]
        r = r_ref[srows, :]
        outs = []
        for h in range(GLA_HEADS):
            oh = o_blk[:, h * GLA_DV:(h + 1) * GLA_DV]
            ms = jnp.mean(oh * oh, axis=-1, keepdims=True)
            outs.append(oh * lax.rsqrt(ms + RMS_EPS) * g_ref[...])
        y = jnp.concatenate(outs, axis=1)
        o_ref[srows, :] = (y * (r * jax.nn.sigmoid(r))).astype(o_ref.dtype)
        return c

    lax.fori_loop(0, nsub, sub_body, 0)


def _gla_prompt_kernel(q_ref, k_ref, v_ref, la_ref, r_ref, g_ref, o_ref, sout_ref,
                       s_ref, b_scr, o_scr, *, chunk, sub, n_chunks):
    @pl.when(pl.program_id(1) == 0)
    def _():
        s_ref[...] = jnp.zeros_like(s_ref)

    def body(c, carry):
        _gla_chunk(q_ref, k_ref, v_ref, la_ref, r_ref, g_ref, o_ref, s_ref, b_scr, o_scr,
                   pl.multiple_of(c * chunk, chunk), chunk, sub)
        return carry

    lax.fori_loop(0, n_chunks, body, 0)
    sout_ref[...] = s_ref[...]


def _gla_prompt(gq, gk, gv, la, gr, g, batch, seq):
    step = min(GLA_STEP, seq)
    chunk = min(GLA_CHUNK, seq)
    sub = min(GLA_SUB, chunk)
    assert seq % step == 0 and step % chunk == 0 and chunk % sub == 0
    ns = seq // step
    row = lambda c: pl.BlockSpec((step, c), lambda b, i: (b * ns + i, 0))
    kern = functools.partial(_gla_prompt_kernel, chunk=chunk, sub=sub, n_chunks=step // chunk)
    return pl.pallas_call(
        kern,
        grid=(batch, ns),
        in_specs=[row(GLA_K_WIDTH), row(GLA_K_WIDTH), row(GLA_V_WIDTH), row(GLA_K_WIDTH),
                  row(GLA_V_WIDTH), _const_spec((1, GLA_DV))],
        out_specs=[row(GLA_V_WIDTH),
                   pl.BlockSpec((None, GLA_K_WIDTH, GLA_DV), lambda b, i: (b, 0, 0))],
        out_shape=[jax.ShapeDtypeStruct((batch * seq, GLA_V_WIDTH), BF16),
                   jax.ShapeDtypeStruct((batch, GLA_K_WIDTH, GLA_DV), F32)],
        scratch_shapes=[pltpu.VMEM((GLA_K_WIDTH, GLA_DV), F32),
                        pltpu.VMEM((chunk, GLA_K_WIDTH), F32),
                        pltpu.VMEM((chunk, GLA_V_WIDTH), F32)],
        compiler_params=_params(("parallel", "arbitrary")),
        name="gla_prompt",
    )(gq, gk, gv, la, gr, g)


def _gla_sample_kernel(q_ref, k_ref, v_ref, la_ref, r_ref, g_ref, s0_ref, o_ref, sout_ref,
                       s_ref, b_scr, o_scr, *, chunk):
    s_ref[...] = s0_ref[...]
    _gla_chunk(q_ref, k_ref, v_ref, la_ref, r_ref, g_ref, o_ref, s_ref, b_scr, o_scr,
               0, chunk, chunk)
    sout_ref[...] = s_ref[...]


def _gla_sample(gq, gk, gv, la, gr, g, s0, row_off, nseq, ts):
    assert row_off % ts == 0 and ts <= GLA_CHUNK and ts % SUBLANES == 0
    blk0 = row_off // ts
    row = lambda c: pl.BlockSpec((ts, c), lambda b: (blk0 + b, 0))
    st = pl.BlockSpec((None, GLA_K_WIDTH, GLA_DV), lambda b: (b, 0, 0))
    kern = functools.partial(_gla_sample_kernel, chunk=ts)
    return pl.pallas_call(
        kern,
        grid=(nseq,),
        in_specs=[row(GLA_K_WIDTH), row(GLA_K_WIDTH), row(GLA_V_WIDTH), row(GLA_K_WIDTH),
                  row(GLA_V_WIDTH), _const_spec((1, GLA_DV)), st],
        out_specs=[pl.BlockSpec((ts, GLA_V_WIDTH), lambda b: (b, 0)), st],
        out_shape=[jax.ShapeDtypeStruct((nseq * ts, GLA_V_WIDTH), BF16),
                   jax.ShapeDtypeStruct((nseq, GLA_K_WIDTH, GLA_DV), F32)],
        scratch_shapes=[pltpu.VMEM((GLA_K_WIDTH, GLA_DV), F32),
                        pltpu.VMEM((ts, GLA_K_WIDTH), F32),
                        pltpu.VMEM((ts, GLA_V_WIDTH), F32)],
        compiler_params=_params(("parallel",)),
        name="gla_sample",
    )(gq, gk, gv, la, gr, g, s0)


def _head_pair_perm():
    cols = []
    for p in range(ATTN_HEADS // ATTN_KV_HEADS):
        for g in range(ATTN_KV_HEADS):
            h = p + g * (ATTN_HEADS // ATTN_KV_HEADS)
            cols.extend(range(h * HEAD_DIM, (h + 1) * HEAD_DIM))
    return np.asarray(cols, np.int32)


def _prep_w_in(w):
    d = w.shape[0]
    splits = np.cumsum([ATTN_WIDTH, KV_WIDTH, KV_WIDTH, IDX_HEADS * IDX_DIM, IDX_DIM, IDX_HEADS,
                        GLA_K_WIDTH, GLA_K_WIDTH, GLA_V_WIDTH, GLA_V_WIDTH, GLA_GATE_RANK])[:-1]
    q, k, v, iq, ik, iw, gq, gk, gv, gr, glr = jnp.split(w, [int(s) for s in splits], axis=1)
    zeros = lambda c: jnp.zeros((d, c), w.dtype)
    parts = [q[:, _head_pair_perm()], k, v, iq, ik, ik, iw, zeros(LANES - IDX_HEADS),
             gq, gk, gv, gr, glr, zeros(LANES - GLA_GATE_RANK)]
    out = jnp.concatenate(parts, axis=1).astype(BF16)
    assert out.shape[1] == C_END
    return out


def kernel(x_prompt, x_sample, cache_k, cache_v, cache_idx_k, state_gla, page_table,
           attn_norm_g, w_in, idx_knorm_g, idx_knorm_b, gla_gate_up, gla_gate_b, gla_onorm_g,
           w_out, ffn_norm_g, w_gate, w_up, w_down, rel_bias, final_norm_g):
    B, T, D = x_prompt.shape
    DB, TS, _ = x_sample.shape
    depth = w_in.shape[0]
    n_pool, page = cache_k.shape[1], cache_k.shape[2]
    n_pages = page_table.shape[1]
    past_len = n_pages * page
    topk_p = min(IDX_TOPK_MAX, T // 4)
    topk_s = min(IDX_TOPK_MAX, (past_len + TS) // 4)
    NP, NS = B * T, DB * TS
    N = NP + NS
    assert N % TOKEN_TILE == 0 and NP % TS == 0

    tq = min(ATTN_TILE, T)
    btab = _bias_tables(rel_bias, tq, tq, offsets=(tq, 0))
    cfar = btab[0, :, tq - 1, 0]
    ktot = past_len + page
    bias_s = _bias_tables(rel_bias, TS, ktot, offsets=(past_len,)).reshape(ATTN_HEADS * TS, ktot)

    ck = cache_k.reshape(depth, n_pool, page, KV_WIDTH)
    cv = cache_v.reshape(depth, n_pool, page, KV_WIDTH)
    perm = _head_pair_perm()
    lane = np.arange(LANES)
    half_mask = np.stack([lane < HEAD_DIM, lane >= HEAD_DIM])

    x = jnp.concatenate([x_prompt.reshape(NP, D), x_sample.reshape(NS, D)], axis=0)
    outs = {n: [] for n in ("kp", "vp", "ikp", "sp", "ks", "vs", "iks", "ss")}
    for l in range(depth):
        w = _prep_w_in(w_in[l])
        lng = jnp.concatenate([idx_knorm_g[l], idx_knorm_g[l]])[None, :]
        lnb = jnp.concatenate([idx_knorm_b[l], idx_knorm_b[l]])[None, :]
        gup = jnp.zeros((LANES, GLA_K_WIDTH), BF16).at[:GLA_GATE_RANK].set(gla_gate_up[l].astype(BF16))
        (q, k32, v32, kb, vb, iq, ik32, ik2, iw, gq, gk, gv, gr, la) = _inproj(
            x, attn_norm_g[l][None, :], w, lng, lnb, gup, gla_gate_b[l][None, :])

        ao_p = _dsa_prompt(iq, iw, q, ik2, kb, vb, btab, cfar, B, T, topk_p)
        onorm = gla_onorm_g[l][None, :]
        go_p, s_p = _gla_prompt(gq, gk, gv, la, gr, onorm, B, T)

        iq_s = iq[NP:].reshape(DB, TS, IDX_HEADS, IDX_DIM).transpose(0, 2, 1, 3)
        iq_s = iq_s.reshape(DB, IDX_HEADS * TS, IDX_DIM)
        iw_s = iw[NP:, :IDX_HEADS].reshape(DB, TS, IDX_HEADS).transpose(0, 2, 1)
        iw_s = jnp.broadcast_to(iw_s.reshape(DB, IDX_HEADS * TS, 1), (DB, IDX_HEADS * TS, LANES))
        q_s = q[NP:].reshape(DB, TS, ATTN_HEADS // 2, LANES)
        q_s = jnp.stack([jnp.where(half_mask[g], q_s, jnp.zeros_like(q_s))
                         for g in range(ATTN_KV_HEADS)], axis=1)
        q_s = q_s.transpose(0, 1, 3, 2, 4).reshape(DB, ATTN_HEADS * TS, LANES)
        pad_rows = lambda a: jnp.pad(a.reshape(DB, TS, a.shape[-1]), ((0, 0), (0, page - TS), (0, 0)))
        ao_s = _dsa_sample(page_table, iq_s, iw_s, q_s, pad_rows(ik32[NP:]), pad_rows(kb[NP:]),
                           pad_rows(vb[NP:]), bias_s, cache_idx_k, ck, cv, l, topk_s)
        ao_s = ao_s.reshape(DB, ATTN_HEADS // 2, TS, LANES).transpose(0, 2, 1, 3).reshape(NS, ATTN_WIDTH)
        s0 = state_gla[l].reshape(DB, GLA_K_WIDTH, GLA_DV)
        go_s, s_s = _gla_sample(gq, gk, gv, la, gr, onorm, s0, NP, DB, TS)

        ao = jnp.concatenate([ao_p, ao_s], axis=0)
        go = jnp.concatenate([go_p, go_s], axis=0)
        wo = jnp.concatenate([w_out[l][:ATTN_WIDTH][perm], w_out[l][ATTN_WIDTH:]], axis=0).astype(BF16)
        x = _ffn(x, ao, go, wo, ffn_norm_g[l][None, :], w_gate[l].astype(BF16),
                 w_up[l].astype(BF16), w_down[l].astype(BF16))

        outs["kp"].append(k32[:NP].reshape(B, T, ATTN_KV_HEADS, HEAD_DIM))
        outs["vp"].append(v32[:NP].reshape(B, T, ATTN_KV_HEADS, HEAD_DIM))
        outs["ikp"].append(ik32[:NP].reshape(B, T, IDX_DIM))
        outs["sp"].append(s_p.reshape(B, GLA_HEADS, GLA_DK, GLA_DV))
        outs["ks"].append(k32[NP:].reshape(DB, TS, ATTN_KV_HEADS, HEAD_DIM))
        outs["vs"].append(v32[NP:].reshape(DB, TS, ATTN_KV_HEADS, HEAD_DIM))
        outs["iks"].append(ik32[NP:].reshape(DB, TS, IDX_DIM))
        outs["ss"].append(s_s.reshape(DB, GLA_HEADS, GLA_DK, GLA_DV))

    y = _rmsnorm(x, final_norm_g[None, :])
    st = lambda n: jnp.stack(outs[n])
    return (y[:NP].reshape(B, T, D), y[NP:].reshape(DB, TS, D),
            st("kp"), st("vp"), st("ikp"), st("sp"), st("ks"), st("vs"), st("iks"), st("ss"))
```

```python
import functools
import math

import jax
import jax.numpy as jnp
import numpy as np
from jax import lax
from jax.experimental import pallas as pl
from jax.experimental.pallas import tpu as pltpu

F32 = jnp.float32
BF16 = jnp.bfloat16
I32 = jnp.int32

ATTN_HEADS = 8
ATTN_KV_HEADS = 2
HEAD_DIM = 64
ATTN_WIDTH = ATTN_HEADS * HEAD_DIM
KV_WIDTH = ATTN_KV_HEADS * HEAD_DIM
IDX_HEADS = 8
IDX_DIM = 64
IDX_TOPK_MAX = 256
GLA_HEADS = 4
GLA_DK = 64
GLA_DV = 128
GLA_K_WIDTH = GLA_HEADS * GLA_DK
GLA_V_WIDTH = GLA_HEADS * GLA_DV
GLA_GATE_RANK = 16
GLA_GATE_TAU = 16.0
GLA_CHUNK = 64
GLA_SUB = 16
REL_BUCKETS = 32
REL_MAX_DIST = 128
RMS_EPS = 1e-6
LN_EPS = 1e-5

LANES = 128
SUBLANES = 8
VMEM_LIMIT = 56 * 1024 * 1024

TOKEN_TILE = 512
FFN_CHUNK = 256
ATTN_TILE = 256
ROW_BLOCK = 64
FAR_SPAN = 2
GLA_STEP = 512

INT_MIN = np.int32(-2 ** 31)
INT_MAX = np.int32(2 ** 31 - 1)
NEG = -1e30
LOG2E = math.log2(math.e)
Q_SCALE = HEAD_DIM ** -0.5 * LOG2E

C_Q = 0
C_K = 512
C_V = 640
C_IQ = 768
C_IK = 1280
C_IW = 1408
C_GQ = 1536
C_GK = 1792
C_GV = 2048
C_GR = 2560
C_GLR = 3072
C_END = 3200


def _nt(a, b):
    return lax.dot_general(a, b, (((1,), (1,)), ((), ())), preferred_element_type=F32)


def _tn(a, b, precision=None):
    return lax.dot_general(a, b, (((0,), (0,)), ((), ())), preferred_element_type=F32,
                           precision=precision)


def _rep(x, n):
    return x if n == 1 else jnp.concatenate([x] * n, axis=1)


def _const_spec(shape):
    nd = len(shape)
    return pl.BlockSpec(shape, lambda *_: (0,) * nd, pipeline_mode=pl.Buffered(1))


def _params(sem):
    return pltpu.CompilerParams(dimension_semantics=sem, vmem_limit_bytes=VMEM_LIMIT)


def _inproj_kernel(x_ref, g_ref, w_ref, lng_ref, lnb_ref, gup_ref, gb_ref,
                   q_ref, k_ref, v_ref, kb_ref, vb_ref, iq_ref, ik_ref, ik2_ref, iw_ref,
                   gq_ref, gk_ref, gv_ref, gr_ref, la_ref):
    x = x_ref[...]
    ms = jnp.mean(x * x, axis=-1, keepdims=True)
    h = (x * lax.rsqrt(ms + RMS_EPS) * g_ref[...]).astype(BF16)

    def mm(a, b):
        return jnp.dot(h, w_ref[:, a:b], preferred_element_type=F32)

    q_ref[...] = (mm(C_Q, C_K) * Q_SCALE).astype(BF16)
    k = mm(C_K, C_V)
    k_ref[...] = k
    kb_ref[...] = k.astype(BF16)
    v = mm(C_V, C_IQ)
    v_ref[...] = v
    vb_ref[...] = v.astype(BF16)
    iq_ref[...] = mm(C_IQ, C_IK).astype(BF16)

    z = mm(C_IK, C_IW)
    mu = jnp.mean(z, axis=-1, keepdims=True)
    zc = z - mu
    var = jnp.mean(zc * zc, axis=-1, keepdims=True)
    ikn = zc * lax.rsqrt(var + LN_EPS) * lng_ref[...] + lnb_ref[...]
    ik_ref[...] = ikn[:, :IDX_DIM]
    ik2_ref[...] = ikn.astype(BF16)

    iw_ref[...] = mm(C_IW, C_GQ) * (IDX_HEADS ** -0.5 * IDX_DIM ** -0.5)
    gq_ref[...] = mm(C_GQ, C_GK) * (GLA_DK ** -0.5)
    gk_ref[...] = mm(C_GK, C_GV)
    gv_ref[...] = mm(C_GV, C_GR)
    gr_ref[...] = mm(C_GR, C_GLR)
    glr = mm(C_GLR, C_END).astype(BF16)
    gate = jnp.dot(glr, gup_ref[...], preferred_element_type=F32) + gb_ref[...]
    log_sig = jnp.minimum(gate, 0.0) - jnp.log1p(jnp.exp(-jnp.abs(gate)))
    la_ref[...] = log_sig / GLA_GATE_TAU


def _inproj(x, g, w, lng, lnb, gup, gb):
    n, d = x.shape
    tm = TOKEN_TILE
    row = lambda c: pl.BlockSpec((tm, c), lambda i: (i, 0))
    outs = [(ATTN_WIDTH, BF16), (KV_WIDTH, F32), (KV_WIDTH, F32), (KV_WIDTH, BF16), (KV_WIDTH, BF16),
            (IDX_HEADS * IDX_DIM, BF16), (IDX_DIM, F32), (2 * IDX_DIM, BF16), (LANES, F32),
            (GLA_K_WIDTH, F32), (GLA_K_WIDTH, F32), (GLA_V_WIDTH, F32), (GLA_V_WIDTH, F32),
            (GLA_K_WIDTH, F32)]
    return pl.pallas_call(
        _inproj_kernel,
        grid=(n // tm,),
        in_specs=[row(d), _const_spec((1, d)), _const_spec(w.shape), _const_spec(lng.shape),
                  _const_spec(lnb.shape), _const_spec(gup.shape), _const_spec(gb.shape)],
        out_specs=[row(c) for c, _ in outs],
        out_shape=[jax.ShapeDtypeStruct((n, c), dt) for c, dt in outs],
        compiler_params=_params(("parallel",)),
        name="inproj",
    )(x, g, w, lng, lnb, gup, gb)


def _ffn_kernel(x_ref, ao_ref, go_ref, wo_ref, g_ref, wg_ref, wu_ref, wd_ref, o_ref):
    half = ao_ref.shape[1]
    x1 = (x_ref[...]
          + jnp.dot(ao_ref[...], wo_ref[:half, :], preferred_element_type=F32)
          + jnp.dot(go_ref[...], wo_ref[half:, :], preferred_element_type=F32))
    ms = jnp.mean(x1 * x1, axis=-1, keepdims=True)
    h = (x1 * lax.rsqrt(ms + RMS_EPS) * g_ref[...]).astype(BF16)
    acc = None
    d_ff = wg_ref.shape[1]
    for c in range(0, d_ff, FFN_CHUNK):
        gt = jnp.dot(h, wg_ref[:, c:c + FFN_CHUNK], preferred_element_type=F32)
        up = jnp.dot(h, wu_ref[:, c:c + FFN_CHUNK], preferred_element_type=F32)
        u = (gt * jax.nn.sigmoid(gt) * up).astype(BF16)
        part = jnp.dot(u, wd_ref[c:c + FFN_CHUNK, :], preferred_element_type=F32)
        acc = part if acc is None else acc + part
    o_ref[...] = x1 + acc


def _ffn(x, ao, go, wo, g, wg, wu, wd):
    n, d = x.shape
    tm = TOKEN_TILE
    assert wg.shape[1] % FFN_CHUNK == 0
    row = lambda c: pl.BlockSpec((tm, c), lambda i: (i, 0))
    return pl.pallas_call(
        _ffn_kernel,
        grid=(n // tm,),
        in_specs=[row(d), row(ao.shape[1]), row(go.shape[1]), _const_spec(wo.shape),
                  _const_spec((1, d)), _const_spec(wg.shape), _const_spec(wu.shape),
                  _const_spec(wd.shape)],
        out_specs=row(d),
        out_shape=jax.ShapeDtypeStruct((n, d), F32),
        compiler_params=_params(("parallel",)),
        name="outproj_ffn",
    )(x, ao, go, wo, g, wg, wu, wd)


def _rmsnorm_kernel(x_ref, g_ref, o_ref):
    x = x_ref[...]
    ms = jnp.mean(x * x, axis=-1, keepdims=True)
    o_ref[...] = x * lax.rsqrt(ms + RMS_EPS) * g_ref[...]


def _rmsnorm(x, g):
    n, d = x.shape
    tm = TOKEN_TILE
    return pl.pallas_call(
        _rmsnorm_kernel,
        grid=(n // tm,),
        in_specs=[pl.BlockSpec((tm, d), lambda i: (i, 0)), _const_spec((1, d))],
        out_specs=pl.BlockSpec((tm, d), lambda i: (i, 0)),
        out_shape=jax.ShapeDtypeStruct((n, d), F32),
        compiler_params=_params(("parallel",)),
        name="final_norm",
    )(x, g)


def _rel_bucket(dist):
    n = jnp.maximum(dist, 0)
    max_exact = REL_BUCKETS // 2
    large = max_exact + (jnp.log(jnp.maximum(n, 1).astype(F32) / max_exact)
                         / math.log(REL_MAX_DIST / max_exact)
                         * (REL_BUCKETS - max_exact)).astype(I32)
    large = jnp.minimum(large, REL_BUCKETS - 1)
    return jnp.where(n < max_exact, n, large)


def _bias_kernel(rb_ref, o_ref, *, offsets):
    _, nh, r, c = o_ref.shape
    row = lax.broadcasted_iota(I32, (r, c), 0)
    col = lax.broadcasted_iota(I32, (r, c), 1)
    for t, off in enumerate(offsets):
        bucket = _rel_bucket(row - col + off)
        for h in range(nh):
            val = jnp.zeros((r, c), F32)
            for b in range(REL_BUCKETS):
                val = jnp.where(bucket == b, rb_ref[b, h], val)
            o_ref[t, h] = val * LOG2E


def _bias_tables(rel_bias, rows, cols, offsets):
    nh = rel_bias.shape[1]
    return pl.pallas_call(
        functools.partial(_bias_kernel, offsets=tuple(offsets)),
        in_specs=[pl.BlockSpec(memory_space=pltpu.SMEM)],
        out_shape=jax.ShapeDtypeStruct((len(offsets), nh, rows, cols), F32),
        name="bias_tables",
    )(rel_bias)


def _score_key(s):
    bits = lax.bitcast_convert_type(s, I32)
    return bits ^ ((bits >> 31) & INT_MAX)


def _topk_threshold(count_ge, rows, topk):
    def bit_body(it, u):
        bit = lax.shift_left(jnp.int32(1), 31 - it)
        u2 = u | bit
        cnt = count_ge(u2 ^ INT_MIN)
        return jnp.where(cnt >= topk, u2, u)

    u = lax.fori_loop(0, 32, bit_body, jnp.zeros((rows, LANES), I32))
    return u ^ INT_MIN


def _tie_limit(count_eq_below, need, rows, idx_bits):
    def bit_body(it, p):
        bit = lax.shift_left(jnp.int32(1), idx_bits - 1 - it)
        p2 = p | bit
        cnt = count_eq_below(p2)
        return jnp.where(cnt < need, p2, p)

    return lax.fori_loop(0, idx_bits, bit_body, jnp.zeros((rows, LANES), I32))


def _dsa_prompt_kernel(iq_ref, iw_ref, q_ref, ik2_ref, k_ref, v_ref, btab_ref, cfar_ref,
                       o_ref,
                       skey_ref, iqm_ref, qm_ref, iwr_ref, m_ref, l_ref, acc_ref, p_ref,
                       thr_ref, madd_ref, lg0_ref, lg1_ref, pb0_ref, pb1_ref,
                       *, tq, topk, idx_bits):
    i = pl.program_id(1)
    nrep = tq // LANES
    lane = lax.broadcasted_iota(I32, (1, LANES), 1)
    lo_half = lane < HEAD_DIM
    row_id = lax.broadcasted_iota(I32, (tq, tq), 0)
    col_id = lax.broadcasted_iota(I32, (tq, tq), 1)

    for h in range(IDX_HEADS):
        grp = iq_ref[:, LANES * (h // 2):LANES * (h // 2 + 1)]
        iqm_ref[h] = jnp.where(lo_half if h % 2 == 0 else ~lo_half, grp, jnp.zeros_like(grp))
        iwr_ref[h] = jnp.broadcast_to(iw_ref[:, h:h + 1], (tq, LANES))
    for h in range(ATTN_HEADS):
        p, g = h % 4, h // 4
        grp = q_ref[:, LANES * p:LANES * (p + 1)]
        qm_ref[h] = jnp.where(lo_half if g == 0 else ~lo_half, grp, jnp.zeros_like(grp))
        m_ref[h] = jnp.full((tq, LANES), NEG, F32)
        l_ref[h] = jnp.zeros((tq, LANES), F32)
        acc_ref[h] = jnp.zeros((tq, LANES), F32)

    def key_rows(j):
        return pl.ds(pl.multiple_of(j * tq, tq), tq)

    def score_tile(j, diag):
        ikt = ik2_ref[key_rows(j), :]
        s = jnp.zeros((tq, tq), F32)
        for h in range(IDX_HEADS):
            d = _nt(iqm_ref[h], ikt)
            s = s + _rep(iwr_ref[h], nrep) * jnp.maximum(d, 0.0)
        key = _score_key(s)
        if diag:
            key = jnp.where(col_id <= row_id, key, INT_MIN)
        skey_ref[j] = key

    def score_body(j, c):
        score_tile(j, False)
        return c

    lax.fori_loop(0, i, score_body, 0)
    score_tile(i, True)

    rb = ROW_BLOCK
    n_rb = tq // rb

    def count_rows(r0, pred):
        def body(j, acc):
            kt = skey_ref[j, r0:r0 + rb, :]
            for c in range(nrep):
                acc = acc + jnp.where(pred(kt[:, c * LANES:(c + 1) * LANES], j * tq + c * LANES),
                                      1.0, 0.0)
            return acc
        acc = lax.fori_loop(0, i + 1, body, jnp.zeros((rb, LANES), F32))
        return jnp.sum(acc, axis=1, keepdims=True)

    any_tie = jnp.float32(0.0)
    for b in range(n_rb):
        r0 = b * rb
        thr = _topk_threshold(
            lambda cand: count_rows(r0, lambda kt, _: kt >= cand), rb, topk)
        cnt_ge = count_rows(r0, lambda kt, _: kt >= thr)
        real_thr = thr[:, :1] > INT_MIN
        tie = jnp.logical_and(cnt_ge > topk, real_thr)
        thr_ref[r0:r0 + rb, :] = thr
        p_ref[r0:r0 + rb, :] = jnp.broadcast_to(
            jnp.where(real_thr, INT_MAX, jnp.int32(-1)), (rb, LANES))
        any_tie = jnp.maximum(any_tie, jnp.max(jnp.where(tie, 1.0, 0.0)))

    @pl.when(any_tie > 0.0)
    def _():
        for b in range(n_rb):
            r0 = b * rb
            thr = thr_ref[r0:r0 + rb, :]
            cnt_gt = count_rows(r0, lambda kt, _: kt > thr)
            cnt_ge = count_rows(r0, lambda kt, _: kt >= thr)
            tie = jnp.logical_and(cnt_ge > topk, thr[:, :1] > INT_MIN)
            plim = _tie_limit(
                lambda p: count_rows(r0, lambda kt, base: jnp.logical_and(kt == thr,
                                                                           base + lane < p)),
                topk - cnt_gt, rb, idx_bits)
            p_ref[r0:r0 + rb, :] = jnp.where(tie, plim, p_ref[r0:r0 + rb, :])

    col_rb = lax.broadcasted_iota(I32, (rb, tq), 1)

    def attn_span(j0, nt, near):
        width = nt * tq
        kspan = pl.ds(pl.multiple_of(j0 * tq, tq), width)
        kb = k_ref[kspan, :]
        vb = v_ref[kspan, :]

        def mask_body(b, c):
            rows = pl.ds(pl.multiple_of(b * rb, rb), rb)
            thr = _rep(thr_ref[rows, :], nrep)
            for t in range(nt):
                kt = skey_ref[j0 + t, rows, :]
                lim = _rep(p_ref[rows, :] - (j0 + t) * tq, nrep)
                sel = jnp.logical_or(kt > thr, jnp.logical_and(kt == thr, col_rb <= lim))
                madd_ref[rows, t * tq:(t + 1) * tq] = jnp.where(sel, 0.0, NEG)
            return c

        lax.fori_loop(0, n_rb, mask_body, 0)

        bufs = ((lg0_ref, pb0_ref), (lg1_ref, pb1_ref))
        lg0_ref[:, :width] = _nt(qm_ref[0], kb)
        for h in range(ATTN_HEADS):
            lg_ref, pb_ref = bufs[h % 2]
            if h + 1 < ATTN_HEADS:
                bufs[(h + 1) % 2][0][:, :width] = _nt(qm_ref[h + 1], kb)

            parts = []
            for b in range(n_rb):
                r0 = b * rb
                lg = lg_ref[r0:r0 + rb, :width] + madd_ref[r0:r0 + rb, :width]
                if near is not None:
                    lg = lg + (btab_ref[near, h, r0:r0 + rb, :] - cfar_ref[h])
                lg_ref[r0:r0 + rb, :width] = lg
                mx = lg[:, :LANES]
                for cc in range(1, nt * nrep):
                    mx = jnp.maximum(mx, lg[:, cc * LANES:(cc + 1) * LANES])
                parts.append(mx)
            m_old = m_ref[h]
            m_new = jnp.maximum(m_old, jnp.max(jnp.concatenate(parts, axis=0), axis=1,
                                               keepdims=True))
            alpha = jnp.exp2(m_old - m_new)
            m_ref[h] = m_new
            for b in range(n_rb):
                r0 = b * rb
                pexp = jnp.exp2(lg_ref[r0:r0 + rb, :width] - _rep(m_new[r0:r0 + rb], nt * nrep))
                ps = pexp[:, :LANES]
                for cc in range(1, nt * nrep):
                    ps = ps + pexp[:, cc * LANES:(cc + 1) * LANES]
                l_ref[h, r0:r0 + rb, :] = alpha[r0:r0 + rb] * l_ref[h, r0:r0 + rb, :] + ps
                pb_ref[r0:r0 + rb, :width] = pexp.astype(BF16)
            acc_ref[h] = alpha * acc_ref[h] + jnp.dot(pb_ref[:, :width], vb,
                                                      preferred_element_type=F32)

    n_far = jnp.maximum(i - 1, 0)
    n_span = n_far // FAR_SPAN

    def far_span_body(s, c):
        attn_span(s * FAR_SPAN, FAR_SPAN, None)
        return c

    lax.fori_loop(0, n_span, far_span_body, 0)

    def far_body(j, c):
        attn_span(j, 1, None)
        return c

    lax.fori_loop(n_span * FAR_SPAN, n_far, far_body, 0)

    @pl.when(i >= 1)
    def _():
        attn_span(i - 1, 1, 0)

    attn_span(i, 1, 1)

    for p in range(4):
        o_lo = acc_ref[p] / jnp.sum(l_ref[p], axis=1, keepdims=True)
        o_hi = acc_ref[p + 4] / jnp.sum(l_ref[p + 4], axis=1, keepdims=True)
        o_ref[:, LANES * p:LANES * (p + 1)] = jnp.where(lo_half, o_lo, o_hi).astype(o_ref.dtype)


def _dsa_prompt(iq, iw, q, ik2, kb, vb, btab, cfar, batch, seq, topk):
    tq = min(ATTN_TILE, seq)
    assert seq % tq == 0 and tq % LANES == 0
    nq = seq // tq
    idx_bits = max(1, int(seq).bit_length())
    qrow = lambda c: pl.BlockSpec((tq, c), lambda b, i: (b * nq + i, 0))
    seqblk = lambda c: pl.BlockSpec((seq, c), lambda b, i: (b, 0))
    kern = functools.partial(_dsa_prompt_kernel, tq=tq, topk=float(topk), idx_bits=idx_bits)
    return pl.pallas_call(
        kern,
        grid=(batch, nq),
        in_specs=[qrow(IDX_HEADS * IDX_DIM), qrow(LANES), qrow(ATTN_WIDTH),
                  seqblk(2 * IDX_DIM), seqblk(KV_WIDTH), seqblk(KV_WIDTH),
                  _const_spec(btab.shape), pl.BlockSpec(memory_space=pltpu.SMEM)],
        out_specs=qrow(ATTN_WIDTH),
        out_shape=jax.ShapeDtypeStruct((batch * seq, ATTN_WIDTH), BF16),
        scratch_shapes=[pltpu.VMEM((nq, tq, tq), I32),
                        pltpu.VMEM((IDX_HEADS, tq, LANES), BF16),
                        pltpu.VMEM((ATTN_HEADS, tq, LANES), BF16),
                        pltpu.VMEM((IDX_HEADS, tq, LANES), F32),
                        pltpu.VMEM((ATTN_HEADS, tq, LANES), F32),
                        pltpu.VMEM((ATTN_HEADS, tq, LANES), F32),
                        pltpu.VMEM((ATTN_HEADS, tq, LANES), F32),
                        pltpu.VMEM((tq, LANES), I32),
                        pltpu.VMEM((tq, LANES), I32),
                        pltpu.VMEM((tq, FAR_SPAN * tq), F32),
                        pltpu.VMEM((tq, FAR_SPAN * tq), F32),
                        pltpu.VMEM((tq, FAR_SPAN * tq), F32),
                        pltpu.VMEM((tq, FAR_SPAN * tq), BF16),
                        pltpu.VMEM((tq, FAR_SPAN * tq), BF16)],
        compiler_params=_params(("parallel", "arbitrary")),
        name="dsa_prompt",
    )(iq, iw, q, ik2, kb, vb, btab, cfar)


def _dsa_sample_kernel(pt_ref, iq_ref, iwr_ref, q_ref, ikn_ref, kn_ref, vn_ref, bias_ref, *rest,
                       n_pages, page, ts, topk, idx_bits):
    ikp = rest[:n_pages]
    kp = rest[n_pages:2 * n_pages]
    vp = rest[2 * n_pages:3 * n_pages]
    o_ref = rest[3 * n_pages]
    skey_ref, lg_ref = rest[3 * n_pages + 1:]
    del pt_ref
    nt = n_pages + 1
    lane = lax.broadcasted_iota(I32, (1, LANES), 1)
    lo_half = lane < HEAD_DIM
    iq = iq_ref[...]
    iwr = iwr_ref[...]
    qh = q_ref[...]

    def score(keys_bf):
        d = _nt(iq, keys_bf)
        w = iwr * jnp.maximum(d, 0.0)
        s = w[0:ts]
        for h in range(1, IDX_HEADS):
            s = s + w[h * ts:(h + 1) * ts]
        return _score_key(s)

    for pg in range(n_pages):
        skey_ref[:, pg * page:(pg + 1) * page] = score(ikp[pg][...].astype(BF16))
    new_key = score(ikn_ref[...].astype(BF16))
    qrow = lax.broadcasted_iota(I32, (ts, page), 0)
    qcol = lax.broadcasted_iota(I32, (ts, page), 1)
    skey_ref[:, n_pages * page:] = jnp.where(qcol <= qrow, new_key, INT_MIN)

    def count_tiles(pred):
        acc = jnp.zeros((ts, LANES), F32)
        for c in range(nt):
            acc = acc + jnp.where(pred(skey_ref[:, c * LANES:(c + 1) * LANES], c * LANES), 1.0, 0.0)
        return jnp.sum(acc, axis=1, keepdims=True)

    thr = _topk_threshold(lambda cand: count_tiles(lambda kt, _: kt >= cand), ts, topk)
    cnt_gt = count_tiles(lambda kt, _: kt > thr)
    cnt_ge = count_tiles(lambda kt, _: kt >= thr)
    need = topk - cnt_gt
    real_thr = thr[:, :1] > INT_MIN
    tie = jnp.logical_and(cnt_ge > topk, real_thr)

    def count_eq_below(p):
        return count_tiles(lambda kt, base: jnp.logical_and(kt == thr, base + lane < p))

    plim = _tie_limit(count_eq_below, need, ts, idx_bits)
    plim = jnp.where(tie, plim,
                     jnp.broadcast_to(jnp.where(real_thr, INT_MAX, jnp.int32(-1)), (ts, LANES)))

    ktot = nt * page
    for pg in range(n_pages):
        lg_ref[:, pg * page:(pg + 1) * page] = _nt(qh, kp[pg][...].astype(BF16))
    lg_ref[:, n_pages * page:] = _nt(qh, kn_ref[...])
    kt = skey_ref[...]
    thr_full = _rep(thr, ktot // LANES)
    idx = lax.broadcasted_iota(I32, (ts, ktot), 1)
    sel = jnp.logical_or(kt > thr_full,
                         jnp.logical_and(kt == thr_full, idx <= _rep(plim, ktot // LANES)))
    sel = jnp.concatenate([sel.astype(F32)] * ATTN_HEADS, axis=0) > 0.5
    lg = lg_ref[...] + bias_ref[...]
    lg = jnp.where(sel, lg, NEG)
    m = jnp.max(lg, axis=1, keepdims=True)
    pexp = jnp.exp2(lg - m)
    denom = jnp.sum(pexp, axis=1, keepdims=True)
    pb = pexp.astype(BF16)
    acc = jnp.dot(pb[:, n_pages * page:], vn_ref[...], preferred_element_type=F32)
    for pg in range(n_pages):
        acc = acc + jnp.dot(pb[:, pg * page:(pg + 1) * page], vp[pg][...].astype(BF16),
                            preferred_element_type=F32)
    o = acc / denom
    half = (ATTN_HEADS // 2) * ts
    o_ref[...] = jnp.where(lo_half, o[:half], o[half:]).astype(o_ref.dtype)


def _dsa_sample(page_table, iq, iwr, qh, ikn, kn, vn, bias, cidx, ck, cv, layer, topk):
    nseq, n_pages = page_table.shape
    page = cidx.shape[2]
    ts = iq.shape[1] // IDX_HEADS
    assert page == LANES
    idx_bits = int((n_pages + 1) * page).bit_length()
    seq3 = lambda a: pl.BlockSpec((None,) + a.shape[1:], lambda b, pt: (b, 0, 0))

    def page_spec(a, pg):
        return pl.BlockSpec((None, None) + a.shape[2:], lambda b, pt: (layer, pt[b, pg], 0, 0))

    kern = functools.partial(_dsa_sample_kernel, n_pages=n_pages, page=page, ts=ts,
                             topk=float(topk), idx_bits=idx_bits)
    ktot = (n_pages + 1) * page
    in_specs = ([seq3(iq), seq3(iwr), seq3(qh), seq3(ikn), seq3(kn), seq3(vn),
                 pl.BlockSpec(bias.shape, lambda b, pt: (0, 0))]
                + [page_spec(cidx, pg) for pg in range(n_pages)]
                + [page_spec(ck, pg) for pg in range(n_pages)]
                + [page_spec(cv, pg) for pg in range(n_pages)])
    rows_out = (ATTN_HEADS // 2) * ts
    return pl.pallas_call(
        kern,
        grid_spec=pltpu.PrefetchScalarGridSpec(
            num_scalar_prefetch=1,
            grid=(nseq,),
            in_specs=in_specs,
            out_specs=pl.BlockSpec((None, rows_out, LANES), lambda b, pt: (b, 0, 0)),
            scratch_shapes=[pltpu.VMEM((ts, ktot), I32),
                            pltpu.VMEM((ATTN_HEADS * ts, ktot), F32)]),
        out_shape=jax.ShapeDtypeStruct((nseq, rows_out, LANES), BF16),
        compiler_params=_params(("arbitrary",)),
        name="dsa_sample",
    )(page_table, iq, iwr, qh, ikn, kn, vn, bias,
      *([cidx] * n_pages), *([ck] * n_pages), *([cv] * n_pages))


def _gla_chunk(q_ref, k_ref, v_ref, la_ref, r_ref, g_ref, o_ref, s_ref, b_scr, o_scr,
               row0, chunk, sub):
    kw = GLA_K_WIDTH
    rows = pl.ds(row0, chunk)
    q = q_ref[rows, :]
    k = k_ref[rows, :]
    v = v_ref[rows, :]
    la = la_ref[rows, :]
    hi = lax.Precision.HIGHEST
    tri = (lax.broadcasted_iota(I32, (chunk, chunk), 0)
           >= lax.broadcasted_iota(I32, (chunk, chunk), 1)).astype(F32)
    b = jnp.dot(tri, la, preferred_element_type=F32, precision=hi)
    b_scr[...] = b
    lane_head = lax.broadcasted_iota(I32, (1, kw), 1) // GLA_DK
    row_head = lax.broadcasted_iota(I32, (kw, GLA_DV), 0) // GLA_DK
    s_old = s_ref[...]
    s_bf = s_old.astype(BF16)

    qe = q * jnp.exp(b)
    b_last_col = _tn(la, jnp.ones((chunk, GLA_DV), F32), precision=hi)
    b_last = b[chunk - 1:chunk, :]
    ke = (k * jnp.exp(b_last - b)).astype(BF16)
    s_new = jnp.exp(b_last_col) * s_old
    for h in range(GLA_HEADS):
        qh = jnp.where(lane_head == h, qe, 0.0).astype(BF16)
        o_scr[:, h * GLA_DV:(h + 1) * GLA_DV] = jnp.dot(qh, s_bf, preferred_element_type=F32)
        kv = _tn(ke, v[:, h * GLA_DV:(h + 1) * GLA_DV].astype(BF16))
        s_new = s_new + jnp.where(row_head == h, kv, 0.0)
    s_ref[...] = s_new

    nsub = chunk // sub
    seg = (lax.broadcasted_iota(I32, (kw, GLA_V_WIDTH), 0) // GLA_DK
           == lax.broadcasted_iota(I32, (kw, GLA_V_WIDTH), 1) // GLA_DV).astype(BF16)
    t_id = lax.broadcasted_iota(I32, (sub, kw), 0)
    v_bf = v.astype(BF16)
    col = lax.broadcasted_iota(I32, (sub, chunk), 1)

    def sub_body(i, c):
        r0 = pl.multiple_of(i * sub, sub)
        srows = pl.ds(row0 + r0, sub)
        qb = q_ref[srows, :]
        kb = k_ref[srows, :]
        vb = v_ref[srows, :]
        bb = b_scr[pl.ds(r0, sub), :]
        base = bb[0:1, :] - la_ref[pl.ds(row0 + r0, 1), :]
        o_blk = o_scr[pl.ds(r0, sub), :]
        o_h = [o_blk[:, h * GLA_DV:(h + 1) * GLA_DV] for h in range(GLA_HEADS)]
        if nsub > 1:
            qi = qb * jnp.exp(bb - base)
            kpre = (k * jnp.exp(jnp.minimum(base - b, 0.0))).astype(BF16)
            for h in range(GLA_HEADS):
                qih = jnp.where(lane_head == h, qi, 0.0).astype(BF16)
                a = jnp.where(col < r0, _nt(qih, kpre), 0.0).astype(BF16)
                o_h[h] = o_h[h] + jnp.dot(a, v_bf[:, h * GLA_DV:(h + 1) * GLA_DV],
                                          preferred_element_type=F32)
        o_blk = jnp.concatenate(o_h, axis=1)
        prods = []
        for s in range(sub):
            e = jnp.exp(jnp.minimum(bb - bb[s:s + 1, :], 0.0))
            prods.append(jnp.where(t_id >= s, qb * kb[s:s + 1, :] * e, 0.0))
        pall = jnp.concatenate(prods, axis=0).astype(BF16)
        abc = jnp.dot(pall, seg, preferred_element_type=F32)
        for s in range(sub):
            o_blk = o_blk + abc[s * sub:(s + 1) * sub, :] * vb[s:s + 1, :]
        r = r_ref[srows, :]
        outs = []
        for h in range(GLA_HEADS):
            oh = o_blk[:, h * GLA_DV:(h + 1) * GLA_DV]
            ms = jnp.mean(oh * oh, axis=-1, keepdims=True)
            outs.append(oh * lax.rsqrt(ms + RMS_EPS) * g_ref[...])
        y = jnp.concatenate(outs, axis=1)
        o_ref[srows, :] = (y * (r * jax.nn.sigmoid(r))).astype(o_ref.dtype)
        return c

    lax.fori_loop(0, nsub, sub_body, 0)


def _gla_prompt_kernel(q_ref, k_ref, v_ref, la_ref, r_ref, g_ref, o_ref, sout_ref,
                       s_ref, b_scr, o_scr, *, chunk, sub, n_chunks):
    @pl.when(pl.program_id(1) == 0)
    def _():
        s_ref[...] = jnp.zeros_like(s_ref)

    def body(c, carry):
        _gla_chunk(q_ref, k_ref, v_ref, la_ref, r_ref, g_ref, o_ref, s_ref, b_scr, o_scr,
                   pl.multiple_of(c * chunk, chunk), chunk, sub)
        return carry

    lax.fori_loop(0, n_chunks, body, 0)
    sout_ref[...] = s_ref[...]


def _gla_prompt(gq, gk, gv, la, gr, g, batch, seq):
    step = min(GLA_STEP, seq)
    chunk = min(GLA_CHUNK, seq)
    sub = min(GLA_SUB, chunk)
    assert seq % step == 0 and step % chunk == 0 and chunk % sub == 0
    ns = seq // step
    row = lambda c: pl.BlockSpec((step, c), lambda b, i: (b * ns + i, 0))
    kern = functools.partial(_gla_prompt_kernel, chunk=chunk, sub=sub, n_chunks=step // chunk)
    return pl.pallas_call(
        kern,
        grid=(batch, ns),
        in_specs=[row(GLA_K_WIDTH), row(GLA_K_WIDTH), row(GLA_V_WIDTH), row(GLA_K_WIDTH),
                  row(GLA_V_WIDTH), _const_spec((1, GLA_DV))],
        out_specs=[row(GLA_V_WIDTH),
                   pl.BlockSpec((None, GLA_K_WIDTH, GLA_DV), lambda b, i: (b, 0, 0))],
        out_shape=[jax.ShapeDtypeStruct((batch * seq, GLA_V_WIDTH), BF16),
                   jax.ShapeDtypeStruct((batch, GLA_K_WIDTH, GLA_DV), F32)],
        scratch_shapes=[pltpu.VMEM((GLA_K_WIDTH, GLA_DV), F32),
                        pltpu.VMEM((chunk, GLA_K_WIDTH), F32),
                        pltpu.VMEM((chunk, GLA_V_WIDTH), F32)],
        compiler_params=_params(("parallel", "arbitrary")),
        name="gla_prompt",
    )(gq, gk, gv, la, gr, g)


def _gla_sample_kernel(q_ref, k_ref, v_ref, la_ref, r_ref, g_ref, s0_ref, o_ref, sout_ref,
                       s_ref, b_scr, o_scr, *, chunk):
    s_ref[...] = s0_ref[...]
    _gla_chunk(q_ref, k_ref, v_ref, la_ref, r_ref, g_ref, o_ref, s_ref, b_scr, o_scr,
               0, chunk, chunk)
    sout_ref[...] = s_ref[...]


def _gla_sample(gq, gk, gv, la, gr, g, s0, row_off, nseq, ts):
    assert row_off % ts == 0 and ts <= GLA_CHUNK and ts % SUBLANES == 0
    blk0 = row_off // ts
    row = lambda c: pl.BlockSpec((ts, c), lambda b: (blk0 + b, 0))
    st = pl.BlockSpec((None, GLA_K_WIDTH, GLA_DV), lambda b: (b, 0, 0))
    kern = functools.partial(_gla_sample_kernel, chunk=ts)
    return pl.pallas_call(
        kern,
        grid=(nseq,),
        in_specs=[row(GLA_K_WIDTH), row(GLA_K_WIDTH), row(GLA_V_WIDTH), row(GLA_K_WIDTH),
                  row(GLA_V_WIDTH), _const_spec((1, GLA_DV)), st],
        out_specs=[pl.BlockSpec((ts, GLA_V_WIDTH), lambda b: (b, 0)), st],
        out_shape=[jax.ShapeDtypeStruct((nseq * ts, GLA_V_WIDTH), BF16),
                   jax.ShapeDtypeStruct((nseq, GLA_K_WIDTH, GLA_DV), F32)],
        scratch_shapes=[pltpu.VMEM((GLA_K_WIDTH, GLA_DV), F32),
                        pltpu.VMEM((ts, GLA_K_WIDTH), F32),
                        pltpu.VMEM((ts, GLA_V_WIDTH), F32)],
        compiler_params=_params(("parallel",)),
        name="gla_sample",
    )(gq, gk, gv, la, gr, g, s0)


def _head_pair_perm():
    cols = []
    for p in range(ATTN_HEADS // ATTN_KV_HEADS):
        for g in range(ATTN_KV_HEADS):
            h = p + g * (ATTN_HEADS // ATTN_KV_HEADS)
            cols.extend(range(h * HEAD_DIM, (h + 1) * HEAD_DIM))
    return np.asarray(cols, np.int32)


def _prep_w_in(w):
    d = w.shape[0]
    splits = np.cumsum([ATTN_WIDTH, KV_WIDTH, KV_WIDTH, IDX_HEADS * IDX_DIM, IDX_DIM, IDX_HEADS,
                        GLA_K_WIDTH, GLA_K_WIDTH, GLA_V_WIDTH, GLA_V_WIDTH, GLA_GATE_RANK])[:-1]
    q, k, v, iq, ik, iw, gq, gk, gv, gr, glr = jnp.split(w, [int(s) for s in splits], axis=1)
    zeros = lambda c: jnp.zeros((d, c), w.dtype)
    parts = [q[:, _head_pair_perm()], k, v, iq, ik, ik, iw, zeros(LANES - IDX_HEADS),
             gq, gk, gv, gr, glr, zeros(LANES - GLA_GATE_RANK)]
    out = jnp.concatenate(parts, axis=1).astype(BF16)
    assert out.shape[1] == C_END
    return out


def kernel(x_prompt, x_sample, cache_k, cache_v, cache_idx_k, state_gla, page_table,
           attn_norm_g, w_in, idx_knorm_g, idx_knorm_b, gla_gate_up, gla_gate_b, gla_onorm_g,
           w_out, ffn_norm_g, w_gate, w_up, w_down, rel_bias, final_norm_g):
    B, T, D = x_prompt.shape
    DB, TS, _ = x_sample.shape
    depth = w_in.shape[0]
    n_pool, page = cache_k.shape[1], cache_k.shape[2]
    n_pages = page_table.shape[1]
    past_len = n_pages * page
    topk_p = min(IDX_TOPK_MAX, T // 4)
    topk_s = min(IDX_TOPK_MAX, (past_len + TS) // 4)
    NP, NS = B * T, DB * TS
    N = NP + NS
    assert N % TOKEN_TILE == 0 and NP % TS == 0

    tq = min(ATTN_TILE, T)
    btab = _bias_tables(rel_bias, tq, tq, offsets=(tq, 0))
    cfar = btab[0, :, tq - 1, 0]
    ktot = past_len + page
    bias_s = _bias_tables(rel_bias, TS, ktot, offsets=(past_len,)).reshape(ATTN_HEADS * TS, ktot)

    ck = cache_k.reshape(depth, n_pool, page, KV_WIDTH)
    cv = cache_v.reshape(depth, n_pool, page, KV_WIDTH)
    perm = _head_pair_perm()
    lane = np.arange(LANES)
    half_mask = np.stack([lane < HEAD_DIM, lane >= HEAD_DIM])

    x = jnp.concatenate([x_prompt.reshape(NP, D), x_sample.reshape(NS, D)], axis=0)
    outs = {n: [] for n in ("kp", "vp", "ikp", "sp", "ks", "vs", "iks", "ss")}
    for l in range(depth):
        w = _prep_w_in(w_in[l])
        lng = jnp.concatenate([idx_knorm_g[l], idx_knorm_g[l]])[None, :]
        lnb = jnp.concatenate([idx_knorm_b[l], idx_knorm_b[l]])[None, :]
        gup = jnp.zeros((LANES, GLA_K_WIDTH), BF16).at[:GLA_GATE_RANK].set(gla_gate_up[l].astype(BF16))
        (q, k32, v32, kb, vb, iq, ik32, ik2, iw, gq, gk, gv, gr, la) = _inproj(
            x, attn_norm_g[l][None, :], w, lng, lnb, gup, gla_gate_b[l][None, :])

        ao_p = _dsa_prompt(iq, iw, q, ik2, kb, vb, btab, cfar, B, T, topk_p)
        onorm = gla_onorm_g[l][None, :]
        go_p, s_p = _gla_prompt(gq, gk, gv, la, gr, onorm, B, T)

        iq_s = iq[NP:].reshape(DB, TS, IDX_HEADS, IDX_DIM).transpose(0, 2, 1, 3)
        iq_s = iq_s.reshape(DB, IDX_HEADS * TS, IDX_DIM)
        iw_s = iw[NP:, :IDX_HEADS].reshape(DB, TS, IDX_HEADS).transpose(0, 2, 1)
        iw_s = jnp.broadcast_to(iw_s.reshape(DB, IDX_HEADS * TS, 1), (DB, IDX_HEADS * TS, LANES))
        q_s = q[NP:].reshape(DB, TS, ATTN_HEADS // 2, LANES)
        q_s = jnp.stack([jnp.where(half_mask[g], q_s, jnp.zeros_like(q_s))
                         for g in range(ATTN_KV_HEADS)], axis=1)
        q_s = q_s.transpose(0, 1, 3, 2, 4).reshape(DB, ATTN_HEADS * TS, LANES)
        pad_rows = lambda a: jnp.pad(a.reshape(DB, TS, a.shape[-1]), ((0, 0), (0, page - TS), (0, 0)))
        ao_s = _dsa_sample(page_table, iq_s, iw_s, q_s, pad_rows(ik32[NP:]), pad_rows(kb[NP:]),
                           pad_rows(vb[NP:]), bias_s, cache_idx_k, ck, cv, l, topk_s)
        ao_s = ao_s.reshape(DB, ATTN_HEADS // 2, TS, LANES).transpose(0, 2, 1, 3).reshape(NS, ATTN_WIDTH)
        s0 = state_gla[l].reshape(DB, GLA_K_WIDTH, GLA_DV)
        go_s, s_s = _gla_sample(gq, gk, gv, la, gr, onorm, s0, NP, DB, TS)

        ao = jnp.concatenate([ao_p, ao_s], axis=0)
        go = jnp.concatenate([go_p, go_s], axis=0)
        wo = jnp.concatenate([w_out[l][:ATTN_WIDTH][perm], w_out[l][ATTN_WIDTH:]], axis=0).astype(BF16)
        x = _ffn(x, ao, go, wo, ffn_norm_g[l][None, :], w_gate[l].astype(BF16),
                 w_up[l].astype(BF16), w_down[l].astype(BF16))

        outs["kp"].append(k32[:NP].reshape(B, T, ATTN_KV_HEADS, HEAD_DIM))
        outs["vp"].append(v32[:NP].reshape(B, T, ATTN_KV_HEADS, HEAD_DIM))
        outs["ikp"].append(ik32[:NP].reshape(B, T, IDX_DIM))
        outs["sp"].append(s_p.reshape(B, GLA_HEADS, GLA_DK, GLA_DV))
        outs["ks"].append(k32[NP:].reshape(DB, TS, ATTN_KV_HEADS, HEAD_DIM))
        outs["vs"].append(v32[NP:].reshape(DB, TS, ATTN_KV_HEADS, HEAD_DIM))
        outs["iks"].append(ik32[NP:].reshape(DB, TS, IDX_DIM))
        outs["ss"].append(s_s.reshape(DB, GLA_HEADS, GLA_DK, GLA_DV))

    y = _rmsnorm(x, final_norm_g[None, :])
    st = lambda n: jnp.stack(outs[n])
    return (y[:NP].reshape(B, T, D), y[NP:].reshape(DB, TS, D),
            st("kp"), st("vp"), st("ikp"), st("sp"), st("ks"), st("vs"), st("iks"), st("ss"))
```

```python
import functools
import math

import jax
import jax.numpy as jnp
import numpy as np
from jax import lax
from jax.experimental import pallas as pl
from jax.experimental.pallas import tpu as pltpu

F32 = jnp.float32
BF16 = jnp.bfloat16
I32 = jnp.int32

ATTN_HEADS = 8
ATTN_KV_HEADS = 2
HEAD_DIM = 64
ATTN_WIDTH = ATTN_HEADS * HEAD_DIM
KV_WIDTH = ATTN_KV_HEADS * HEAD_DIM
IDX_HEADS = 8
IDX_DIM = 64
IDX_TOPK_MAX = 256
GLA_HEADS = 4
GLA_DK = 64
GLA_DV = 128
GLA_K_WIDTH = GLA_HEADS * GLA_DK
GLA_V_WIDTH = GLA_HEADS * GLA_DV
GLA_GATE_RANK = 16
GLA_GATE_TAU = 16.0
GLA_CHUNK = 64
GLA_SUB = 16
REL_BUCKETS = 32
REL_MAX_DIST = 128
RMS_EPS = 1e-6
LN_EPS = 1e-5

LANES = 128
SUBLANES = 8
VMEM_LIMIT = 56 * 1024 * 1024

TOKEN_TILE = 512
FFN_CHUNK = 256
ATTN_TILE = 256
FAR_SPAN = 4
KEY_SLAB = 32
GLA_STEP = 512

INT_MIN = np.int32(-2 ** 31)
INT_MAX = np.int32(2 ** 31 - 1)
NEG = -1e30
LOG2E = math.log2(math.e)
Q_SCALE = HEAD_DIM ** -0.5 * LOG2E

C_Q = 0
C_K = 512
C_V = 640
C_IQ = 768
C_IK = 1280
C_IW = 1408
C_GQ = 1536
C_GK = 1792
C_GV = 2048
C_GR = 2560
C_GLR = 3072
C_END = 3200


def _nt(a, b):
    return lax.dot_general(a, b, (((1,), (1,)), ((), ())), preferred_element_type=F32)


def _tn(a, b, precision=None):
    return lax.dot_general(a, b, (((0,), (0,)), ((), ())), preferred_element_type=F32,
                           precision=precision)


def _rep(x, n):
    return x if n == 1 else jnp.concatenate([x] * n, axis=1)


def _const_spec(shape):
    nd = len(shape)
    return pl.BlockSpec(shape, lambda *_: (0,) * nd, pipeline_mode=pl.Buffered(1))


def _params(sem):
    return pltpu.CompilerParams(dimension_semantics=sem, vmem_limit_bytes=VMEM_LIMIT)


def _inproj_kernel(x_ref, g_ref, w_ref, lng_ref, lnb_ref, gup_ref, gb_ref,
                   q_ref, k_ref, v_ref, kb_ref, vb_ref, iq_ref, ik_ref, ik2_ref, iw_ref,
                   gq_ref, gk_ref, gv_ref, gr_ref, la_ref):
    x = x_ref[...]
    ms = jnp.mean(x * x, axis=-1, keepdims=True)
    h = (x * lax.rsqrt(ms + RMS_EPS) * g_ref[...]).astype(BF16)

    def mm(a, b):
        return jnp.dot(h, w_ref[:, a:b], preferred_element_type=F32)

    q_ref[...] = (mm(C_Q, C_K) * Q_SCALE).astype(BF16)
    k = mm(C_K, C_V)
    k_ref[...] = k
    kb_ref[...] = k.astype(BF16)
    v = mm(C_V, C_IQ)
    v_ref[...] = v
    vb_ref[...] = v.astype(BF16)
    iq_ref[...] = mm(C_IQ, C_IK).astype(BF16)

    z = mm(C_IK, C_IW)
    mu = jnp.mean(z, axis=-1, keepdims=True)
    zc = z - mu
    var = jnp.mean(zc * zc, axis=-1, keepdims=True)
    ikn = zc * lax.rsqrt(var + LN_EPS) * lng_ref[...] + lnb_ref[...]
    ik_ref[...] = ikn[:, :IDX_DIM]
    ik2_ref[...] = ikn.astype(BF16)

    iw_ref[...] = mm(C_IW, C_GQ) * (IDX_HEADS ** -0.5 * IDX_DIM ** -0.5)
    gq_ref[...] = mm(C_GQ, C_GK) * (GLA_DK ** -0.5)
    gk_ref[...] = mm(C_GK, C_GV)
    gv_ref[...] = mm(C_GV, C_GR)
    gr_ref[...] = mm(C_GR, C_GLR)
    glr = mm(C_GLR, C_END).astype(BF16)
    gate = jnp.dot(glr, gup_ref[...], preferred_element_type=F32) + gb_ref[...]
    log_sig = jnp.minimum(gate, 0.0) - jnp.log1p(jnp.exp(-jnp.abs(gate)))
    la_ref[...] = log_sig / GLA_GATE_TAU


def _inproj(x, g, w, lng, lnb, gup, gb):
    n, d = x.shape
    tm = TOKEN_TILE
    row = lambda c: pl.BlockSpec((tm, c), lambda i: (i, 0))
    outs = [(ATTN_WIDTH, BF16), (KV_WIDTH, F32), (KV_WIDTH, F32), (KV_WIDTH, BF16), (KV_WIDTH, BF16),
            (IDX_HEADS * IDX_DIM, BF16), (IDX_DIM, F32), (2 * IDX_DIM, BF16), (LANES, F32),
            (GLA_K_WIDTH, F32), (GLA_K_WIDTH, F32), (GLA_V_WIDTH, F32), (GLA_V_WIDTH, F32),
            (GLA_K_WIDTH, F32)]
    return pl.pallas_call(
        _inproj_kernel,
        grid=(n // tm,),
        in_specs=[row(d), _const_spec((1, d)), _const_spec(w.shape), _const_spec(lng.shape),
                  _const_spec(lnb.shape), _const_spec(gup.shape), _const_spec(gb.shape)],
        out_specs=[row(c) for c, _ in outs],
        out_shape=[jax.ShapeDtypeStruct((n, c), dt) for c, dt in outs],
        compiler_params=_params(("parallel",)),
        name="inproj",
    )(x, g, w, lng, lnb, gup, gb)


def _ffn_kernel(x_ref, ao_ref, go_ref, wo_ref, g_ref, wg_ref, wu_ref, wd_ref, o_ref):
    half = ao_ref.shape[1]
    x1 = (x_ref[...]
          + jnp.dot(ao_ref[...], wo_ref[:half, :], preferred_element_type=F32)
          + jnp.dot(go_ref[...], wo_ref[half:, :], preferred_element_type=F32))
    ms = jnp.mean(x1 * x1, axis=-1, keepdims=True)
    h = (x1 * lax.rsqrt(ms + RMS_EPS) * g_ref[...]).astype(BF16)
    acc = None
    d_ff = wg_ref.shape[1]
    for c in range(0, d_ff, FFN_CHUNK):
        gt = jnp.dot(h, wg_ref[:, c:c + FFN_CHUNK], preferred_element_type=F32)
        up = jnp.dot(h, wu_ref[:, c:c + FFN_CHUNK], preferred_element_type=F32)
        u = (gt * jax.nn.sigmoid(gt) * up).astype(BF16)
        part = jnp.dot(u, wd_ref[c:c + FFN_CHUNK, :], preferred_element_type=F32)
        acc = part if acc is None else acc + part
    o_ref[...] = x1 + acc


def _ffn(x, ao, go, wo, g, wg, wu, wd):
    n, d = x.shape
    tm = TOKEN_TILE
    assert wg.shape[1] % FFN_CHUNK == 0
    row = lambda c: pl.BlockSpec((tm, c), lambda i: (i, 0))
    return pl.pallas_call(
        _ffn_kernel,
        grid=(n // tm,),
        in_specs=[row(d), row(ao.shape[1]), row(go.shape[1]), _const_spec(wo.shape),
                  _const_spec((1, d)), _const_spec(wg.shape), _const_spec(wu.shape),
                  _const_spec(wd.shape)],
        out_specs=row(d),
        out_shape=jax.ShapeDtypeStruct((n, d), F32),
        compiler_params=_params(("parallel",)),
        name="outproj_ffn",
    )(x, ao, go, wo, g, wg, wu, wd)


def _rmsnorm_kernel(x_ref, g_ref, o_ref):
    x = x_ref[...]
    ms = jnp.mean(x * x, axis=-1, keepdims=True)
    o_ref[...] = x * lax.rsqrt(ms + RMS_EPS) * g_ref[...]


def _rmsnorm(x, g):
    n, d = x.shape
    tm = TOKEN_TILE
    return pl.pallas_call(
        _rmsnorm_kernel,
        grid=(n // tm,),
        in_specs=[pl.BlockSpec((tm, d), lambda i: (i, 0)), _const_spec((1, d))],
        out_specs=pl.BlockSpec((tm, d), lambda i: (i, 0)),
        out_shape=jax.ShapeDtypeStruct((n, d), F32),
        compiler_params=_params(("parallel",)),
        name="final_norm",
    )(x, g)


def _rel_bucket(dist):
    n = jnp.maximum(dist, 0)
    max_exact = REL_BUCKETS // 2
    large = max_exact + (jnp.log(jnp.maximum(n, 1).astype(F32) / max_exact)
                         / math.log(REL_MAX_DIST / max_exact)
                         * (REL_BUCKETS - max_exact)).astype(I32)
    large = jnp.minimum(large, REL_BUCKETS - 1)
    return jnp.where(n < max_exact, n, large)


def _bias_kernel(rb_ref, o_ref, *, offsets, key_major):
    _, nh, r, c = o_ref.shape
    row = lax.broadcasted_iota(I32, (r, c), 0)
    col = lax.broadcasted_iota(I32, (r, c), 1)
    for t, off in enumerate(offsets):
        bucket = _rel_bucket((col - row if key_major else row - col) + off)
        for h in range(nh):
            val = jnp.zeros((r, c), F32)
            for b in range(REL_BUCKETS):
                val = jnp.where(bucket == b, rb_ref[b, h], val)
            o_ref[t, h] = val * LOG2E


def _bias_tables(rel_bias, rows, cols, offsets, key_major=False):
    nh = rel_bias.shape[1]
    return pl.pallas_call(
        functools.partial(_bias_kernel, offsets=tuple(offsets), key_major=key_major),
        in_specs=[pl.BlockSpec(memory_space=pltpu.SMEM)],
        out_shape=jax.ShapeDtypeStruct((len(offsets), nh, rows, cols), F32),
        name="bias_tables",
    )(rel_bias)


def _score_key(s):
    bits = lax.bitcast_convert_type(s, I32)
    return bits ^ ((bits >> 31) & INT_MAX)


def _topk_threshold(count_ge, rows, topk):
    def step(it, base):
        shift = 30 - 2 * it
        u = base
        for digit in (1, 2, 3):
            cand = base | lax.shift_left(jnp.int32(digit), shift)
            u = jnp.where(count_ge(cand ^ INT_MIN) >= topk, cand, u)
        return u

    u = lax.fori_loop(0, 16, step, jnp.zeros((rows, LANES), I32))
    return u ^ INT_MIN


def _tie_limit(count_eq_below, need, rows, idx_bits):
    def bit_body(it, p):
        bit = lax.shift_left(jnp.int32(1), idx_bits - 1 - it)
        p2 = p | bit
        cnt = count_eq_below(p2)
        return jnp.where(cnt < need, p2, p)

    return lax.fori_loop(0, idx_bits, bit_body, jnp.zeros((rows, LANES), I32))


def _dsa_prompt_kernel(iq_ref, iwt_ref, q_ref, ik2_ref, k_ref, vt_ref, btab_ref, cfar_ref,
                         o_ref,
                         skey_ref, iqm_ref, qm_ref, m_ref, l_ref, acc_ref, thr_ref, p_ref,
                         madd_ref, lg0_ref, lg1_ref, pb0_ref, pb1_ref,
                         *, tq, topk, idx_bits):
    i = pl.program_id(1)
    sl = SUBLANES
    slab = KEY_SLAB
    lane = lax.broadcasted_iota(I32, (1, LANES), 1)
    lo_half = lane < HEAD_DIM
    key_id = lax.broadcasted_iota(I32, (tq, tq), 0)
    qry_id = lax.broadcasted_iota(I32, (tq, tq), 1)
    slab_id = lax.broadcasted_iota(I32, (slab, tq), 0)

    for h in range(IDX_HEADS):
        grp = iq_ref[:, LANES * (h // 2):LANES * (h // 2 + 1)]
        iqm_ref[h] = jnp.where(lo_half if h % 2 == 0 else ~lo_half, grp, jnp.zeros_like(grp))
    for h in range(ATTN_HEADS):
        p, g = h % 4, h // 4
        grp = q_ref[:, LANES * p:LANES * (p + 1)]
        qm_ref[h] = jnp.where(lo_half if g == 0 else ~lo_half, grp, jnp.zeros_like(grp))
        m_ref[h] = jnp.full((sl, tq), NEG, F32)
        l_ref[h] = jnp.zeros((sl, tq), F32)
        acc_ref[h] = jnp.zeros((KV_WIDTH, tq), F32)

    def key_rows(j, n=1):
        return pl.ds(pl.multiple_of(j * tq, tq), n * tq)

    def score_tile(j, diag):
        ikt = ik2_ref[key_rows(j), :]
        s = jnp.zeros((tq, tq), F32)
        for h in range(IDX_HEADS):
            d = _nt(ikt, iqm_ref[h])
            s = s + iwt_ref[h:h + 1, :] * jnp.maximum(d, 0.0)
        key = _score_key(s)
        if diag:
            key = jnp.where(key_id <= qry_id, key, INT_MIN)
        skey_ref[j] = key

    def score_body(j, c):
        score_tile(j, False)
        return c

    lax.fori_loop(0, i, score_body, 0)
    score_tile(i, True)

    def tall(x):
        return jnp.concatenate([x] * (slab // sl), axis=0)

    def count_keys(pred):
        def body(j, acc):
            for r in range(tq // slab):
                acc = acc + jnp.where(pred(skey_ref[j, r * slab:(r + 1) * slab, :],
                                           j * tq + r * slab), 1.0, 0.0)
            return acc
        acc = lax.fori_loop(0, i + 1, body, jnp.zeros((slab, tq), F32))
        return jnp.broadcast_to(jnp.sum(acc, axis=0, keepdims=True), (sl, tq))

    def bit_body(it, u):
        u2 = u | lax.shift_left(jnp.int32(1), 31 - it)
        cand = tall(u2 ^ INT_MIN)
        return jnp.where(count_keys(lambda kt, _: kt >= cand) >= topk, u2, u)

    thr = lax.fori_loop(0, 32, bit_body, jnp.zeros((sl, tq), I32)) ^ INT_MIN
    thr_t = tall(thr)
    cnt_ge = count_keys(lambda kt, _: kt >= thr_t)
    real_thr = thr > INT_MIN
    tie = jnp.logical_and(cnt_ge > topk, real_thr)
    thr_ref[...] = thr
    p_ref[...] = jnp.where(real_thr, INT_MAX, jnp.int32(-1))

    @pl.when(jnp.max(jnp.where(tie, 1.0, 0.0)) > 0.0)
    def _():
        need = topk - count_keys(lambda kt, _: kt > thr_t)

        def tie_body(it, p):
            p2 = p | lax.shift_left(jnp.int32(1), idx_bits - 1 - it)
            p2_t = tall(p2)
            cnt = count_keys(lambda kt, base: jnp.logical_and(kt == thr_t, base + slab_id < p2_t))
            return jnp.where(cnt < need, p2, p)

        plim = lax.fori_loop(0, idx_bits, tie_body, jnp.zeros((sl, tq), I32))
        p_ref[...] = jnp.where(tie, plim, p_ref[...])

    def attn_span(j0, nt, near):
        nkeys = nt * tq
        kb = k_ref[key_rows(j0, nt), :]
        vt = vt_ref[:, key_rows(j0, nt)]
        thr_t = tall(thr_ref[...])
        plim_t = tall(p_ref[...])

        def mask_body(r, c):
            r0 = pl.multiple_of(r * slab, slab)
            for t in range(nt):
                kt = skey_ref[j0 + t, pl.ds(r0, slab), :]
                idx = (j0 + t) * tq + r0 + slab_id
                sel = jnp.logical_or(kt > thr_t, jnp.logical_and(kt == thr_t, idx <= plim_t))
                madd_ref[pl.ds(t * tq + r0, slab), :] = jnp.where(sel, 0.0, NEG)
            return c

        lax.fori_loop(0, tq // slab, mask_body, 0)

        bufs = ((lg0_ref, pb0_ref), (lg1_ref, pb1_ref))
        chunk = LANES
        per_chunk = chunk // slab

        def logits_chunk(hh, c):
            rows = slice(c * chunk, (c + 1) * chunk)
            bufs[hh % 2][0][rows, :] = _nt(kb[rows], qm_ref[hh])

        for c in range(nkeys // chunk):
            logits_chunk(0, c)
        for h in range(ATTN_HEADS):
            lg_ref, pb_ref = bufs[h % 2]
            mx = jnp.full((slab, tq), NEG, F32)
            for r in range(nkeys // slab):
                if h + 1 < ATTN_HEADS and r % per_chunk == 0:
                    logits_chunk(h + 1, r // per_chunk)
                rows = slice(r * slab, (r + 1) * slab)
                lg = lg_ref[rows, :] + madd_ref[rows, :]
                if near is not None:
                    lg = lg + (btab_ref[near, h, rows, :] - cfar_ref[h])
                lg_ref[rows, :] = lg
                mx = jnp.maximum(mx, lg)
            m_old = m_ref[h]
            m_new = jnp.maximum(m_old, jnp.max(mx, axis=0, keepdims=True))
            alpha = jnp.exp2(m_old - m_new)
            m_ref[h] = m_new
            m_t = tall(m_new)
            ps = jnp.zeros((slab, tq), F32)
            for r in range(nkeys // slab):
                rows = slice(r * slab, (r + 1) * slab)
                pexp = jnp.exp2(lg_ref[rows, :] - m_t)
                ps = ps + pexp
                pb_ref[rows, :] = pexp.astype(BF16)
            psum = ps[0:sl]
            for r in range(1, slab // sl):
                psum = psum + ps[r * sl:(r + 1) * sl]
            l_ref[h] = alpha * l_ref[h] + psum
            acc_ref[h] = alpha[0:1, :] * acc_ref[h] + jnp.dot(vt, pb_ref[:nkeys, :],
                                                              preferred_element_type=F32)

    n_far = jnp.maximum(i - 1, 0)
    n_span = n_far // FAR_SPAN

    def far_span_body(s, c):
        attn_span(s * FAR_SPAN, FAR_SPAN, None)
        return c

    lax.fori_loop(0, n_span, far_span_body, 0)

    done = n_span * FAR_SPAN
    if FAR_SPAN >= 4:
        half = FAR_SPAN // 2
        n_half = (n_far - done) // half

        def far_half_body(s, c):
            attn_span(done + s * half, half, None)
            return c

        lax.fori_loop(0, n_half, far_half_body, 0)
        done = done + n_half * half

    def far_body(j, c):
        attn_span(j, 1, None)
        return c

    lax.fori_loop(done, n_far, far_body, 0)

    @pl.when(i >= 1)
    def _():
        attn_span(i - 1, 1, 0)

    attn_span(i, 1, 1)

    dim_id = lax.broadcasted_iota(I32, (KV_WIDTH, tq), 0)
    for p in range(4):
        o_lo = acc_ref[p] / jnp.sum(l_ref[p], axis=0, keepdims=True)
        o_hi = acc_ref[p + 4] / jnp.sum(l_ref[p + 4], axis=0, keepdims=True)
        o_t = jnp.where(dim_id < HEAD_DIM, o_lo, o_hi)
        o_ref[:, LANES * p:LANES * (p + 1)] = o_t.T.astype(o_ref.dtype)


def _dsa_prompt(iq, iwt, q, ik2, kb, vt, btab, cfar, batch, seq, topk):
    tq = min(ATTN_TILE, seq)
    assert seq % tq == 0 and tq % LANES == 0
    nq = seq // tq
    idx_bits = max(1, int(seq).bit_length())
    span = FAR_SPAN * tq
    qrow = lambda c: pl.BlockSpec((tq, c), lambda b, i: (b * nq + i, 0))
    seqblk = lambda c: pl.BlockSpec((seq, c), lambda b, i: (b, 0))
    kern = functools.partial(_dsa_prompt_kernel, tq=tq, topk=float(topk), idx_bits=idx_bits)
    return pl.pallas_call(
        kern,
        grid=(batch, nq),
        in_specs=[qrow(IDX_HEADS * IDX_DIM),
                  pl.BlockSpec((IDX_HEADS, tq), lambda b, i: (0, b * nq + i)),
                  qrow(ATTN_WIDTH), seqblk(2 * IDX_DIM), seqblk(KV_WIDTH),
                  pl.BlockSpec((KV_WIDTH, seq), lambda b, i: (0, b)),
                  _const_spec(btab.shape), pl.BlockSpec(memory_space=pltpu.SMEM)],
        out_specs=qrow(ATTN_WIDTH),
        out_shape=jax.ShapeDtypeStruct((batch * seq, ATTN_WIDTH), BF16),
        scratch_shapes=[pltpu.VMEM((nq, tq, tq), I32),
                        pltpu.VMEM((IDX_HEADS, tq, LANES), BF16),
                        pltpu.VMEM((ATTN_HEADS, tq, LANES), BF16),
                        pltpu.VMEM((ATTN_HEADS, SUBLANES, tq), F32),
                        pltpu.VMEM((ATTN_HEADS, SUBLANES, tq), F32),
                        pltpu.VMEM((ATTN_HEADS, KV_WIDTH, tq), F32),
                        pltpu.VMEM((SUBLANES, tq), I32),
                        pltpu.VMEM((SUBLANES, tq), I32),
                        pltpu.VMEM((span, tq), F32),
                        pltpu.VMEM((span, tq), F32),
                        pltpu.VMEM((span, tq), F32),
                        pltpu.VMEM((span, tq), BF16),
                        pltpu.VMEM((span, tq), BF16)],
        compiler_params=_params(("parallel", "arbitrary")),
        name="dsa_prompt",
    )(iq, iwt, q, ik2, kb, vt, btab, cfar)


def _dsa_sample_kernel(pt_ref, iq_ref, iwr_ref, q_ref, ikn_ref, kn_ref, vn_ref, bias_ref, *rest,
                       n_pages, page, ts, topk, idx_bits):
    ikp = rest[:n_pages]
    kp = rest[n_pages:2 * n_pages]
    vp = rest[2 * n_pages:3 * n_pages]
    o_ref = rest[3 * n_pages]
    skey_ref, lg_ref, p_ref = rest[3 * n_pages + 1:]
    del pt_ref
    nt = n_pages + 1
    lane = lax.broadcasted_iota(I32, (1, LANES), 1)
    lo_half = lane < HEAD_DIM
    iq = iq_ref[...]
    iwr = iwr_ref[...]
    qh = q_ref[...]

    def score(keys_bf):
        d = _nt(iq, keys_bf)
        w = iwr * jnp.maximum(d, 0.0)
        s = w[0:ts]
        for h in range(1, IDX_HEADS):
            s = s + w[h * ts:(h + 1) * ts]
        return _score_key(s)

    for pg in range(n_pages):
        skey_ref[:, pg * page:(pg + 1) * page] = score(ikp[pg][...].astype(BF16))
    new_key = score(ikn_ref[...].astype(BF16))
    qrow = lax.broadcasted_iota(I32, (ts, page), 0)
    qcol = lax.broadcasted_iota(I32, (ts, page), 1)
    skey_ref[:, n_pages * page:] = jnp.where(qcol <= qrow, new_key, INT_MIN)

    def count_tiles(pred):
        acc = jnp.zeros((ts, LANES), F32)
        for c in range(nt):
            acc = acc + jnp.where(pred(skey_ref[:, c * LANES:(c + 1) * LANES], c * LANES), 1.0, 0.0)
        return jnp.sum(acc, axis=1, keepdims=True)

    thr = _topk_threshold(lambda cand: count_tiles(lambda kt, _: kt >= cand), ts, topk)
    cnt_ge = count_tiles(lambda kt, _: kt >= thr)
    real_thr = thr[:, :1] > INT_MIN
    tie = jnp.logical_and(cnt_ge > topk, real_thr)
    p_ref[...] = jnp.broadcast_to(jnp.where(real_thr, INT_MAX, jnp.int32(-1)), (ts, LANES))

    @pl.when(jnp.max(jnp.where(tie, 1.0, 0.0)) > 0.0)
    def _():
        need = topk - count_tiles(lambda kt, _: kt > thr)

        def count_eq_below(p):
            return count_tiles(lambda kt, base: jnp.logical_and(kt == thr, base + lane < p))

        p_ref[...] = jnp.where(tie, _tie_limit(count_eq_below, need, ts, idx_bits), p_ref[...])

    plim = p_ref[...]

    ktot = nt * page
    for pg in range(n_pages):
        lg_ref[:, pg * page:(pg + 1) * page] = _nt(qh, kp[pg][...].astype(BF16))
    lg_ref[:, n_pages * page:] = _nt(qh, kn_ref[...])
    kt = skey_ref[...]
    thr_full = _rep(thr, ktot // LANES)
    idx = lax.broadcasted_iota(I32, (ts, ktot), 1)
    sel = jnp.logical_or(kt > thr_full,
                         jnp.logical_and(kt == thr_full, idx <= _rep(plim, ktot // LANES)))
    sel = jnp.concatenate([sel.astype(F32)] * ATTN_HEADS, axis=0) > 0.5
    lg = lg_ref[...] + bias_ref[...]
    lg = jnp.where(sel, lg, NEG)
    m = jnp.max(lg, axis=1, keepdims=True)
    pexp = jnp.exp2(lg - m)
    denom = jnp.sum(pexp, axis=1, keepdims=True)
    pb = pexp.astype(BF16)
    acc = jnp.dot(pb[:, n_pages * page:], vn_ref[...], preferred_element_type=F32)
    for pg in range(n_pages):
        acc = acc + jnp.dot(pb[:, pg * page:(pg + 1) * page], vp[pg][...].astype(BF16),
                            preferred_element_type=F32)
    o = acc / denom
    half = (ATTN_HEADS // 2) * ts
    o_ref[...] = jnp.where(lo_half, o[:half], o[half:]).astype(o_ref.dtype)


def _dsa_sample(page_table, iq, iwr, qh, ikn, kn, vn, bias, cidx, ck, cv, layer, topk):
    nseq, n_pages = page_table.shape
    page = cidx.shape[2]
    ts = iq.shape[1] // IDX_HEADS
    assert page == LANES
    idx_bits = int((n_pages + 1) * page).bit_length()
    seq3 = lambda a: pl.BlockSpec((None,) + a.shape[1:], lambda b, pt: (b, 0, 0))

    def page_spec(a, pg):
        return pl.BlockSpec((None, None) + a.shape[2:], lambda b, pt: (layer, pt[b, pg], 0, 0))

    kern = functools.partial(_dsa_sample_kernel, n_pages=n_pages, page=page, ts=ts,
                             topk=float(topk), idx_bits=idx_bits)
    ktot = (n_pages + 1) * page
    in_specs = ([seq3(iq), seq3(iwr), seq3(qh), seq3(ikn), seq3(kn), seq3(vn),
                 pl.BlockSpec(bias.shape, lambda b, pt: (0, 0))]
                + [page_spec(cidx, pg) for pg in range(n_pages)]
                + [page_spec(ck, pg) for pg in range(n_pages)]
                + [page_spec(cv, pg) for pg in range(n_pages)])
    rows_out = (ATTN_HEADS // 2) * ts
    return pl.pallas_call(
        kern,
        grid_spec=pltpu.PrefetchScalarGridSpec(
            num_scalar_prefetch=1,
            grid=(nseq,),
            in_specs=in_specs,
            out_specs=pl.BlockSpec((None, rows_out, LANES), lambda b, pt: (b, 0, 0)),
            scratch_shapes=[pltpu.VMEM((ts, ktot), I32),
                            pltpu.VMEM((ATTN_HEADS * ts, ktot), F32),
                            pltpu.VMEM((ts, LANES), I32)]),
        out_shape=jax.ShapeDtypeStruct((nseq, rows_out, LANES), BF16),
        compiler_params=_params(("arbitrary",)),
        name="dsa_sample",
    )(page_table, iq, iwr, qh, ikn, kn, vn, bias,
      *([cidx] * n_pages), *([ck] * n_pages), *([cv] * n_pages))


def _gla_chunk(q_ref, k_ref, v_ref, la_ref, r_ref, g_ref, o_ref, s_ref, b_scr, o_scr,
               row0, chunk, sub):
    kw = GLA_K_WIDTH
    rows = pl.ds(row0, chunk)
    q = q_ref[rows, :]
    k = k_ref[rows, :]
    v = v_ref[rows, :]
    la = la_ref[rows, :]
    hi = lax.Precision.HIGHEST
    tri = (lax.broadcasted_iota(I32, (chunk, chunk), 0)
           >= lax.broadcasted_iota(I32, (chunk, chunk), 1)).astype(F32)
    b = jnp.dot(tri, la, preferred_element_type=F32, precision=hi)
    b_scr[...] = b
    lane_head = lax.broadcasted_iota(I32, (1, kw), 1) // GLA_DK
    row_head = lax.broadcasted_iota(I32, (kw, GLA_DV), 0) // GLA_DK
    s_old = s_ref[...]
    s_bf = s_old.astype(BF16)

    qe = q * jnp.exp(b)
    b_last_col = _tn(la, jnp.ones((chunk, GLA_DV), F32), precision=hi)
    b_last = b[chunk - 1:chunk, :]
    ke = (k * jnp.exp(b_last - b)).astype(BF16)
    s_new = jnp.exp(b_last_col) * s_old
    for h in range(GLA_HEADS):
        qh = jnp.where(lane_head == h, qe, 0.0).astype(BF16)
        o_scr[:, h * GLA_DV:(h + 1) * GLA_DV] = jnp.dot(qh, s_bf, preferred_element_type=F32)
        kv = _tn(ke, v[:, h * GLA_DV:(h + 1) * GLA_DV].astype(BF16))
        s_new = s_new + jnp.where(row_head == h, kv, 0.0)
    s_ref[...] = s_new

    nsub = chunk // sub
    seg = (lax.broadcasted_iota(I32, (kw, GLA_V_WIDTH), 0) // GLA_DK
           == lax.broadcasted_iota(I32, (kw, GLA_V_WIDTH), 1) // GLA_DV).astype(BF16)
    t_id = lax.broadcasted_iota(I32, (sub, kw), 0)
    v_bf = v.astype(BF16)
    col = lax.broadcasted_iota(I32, (sub, chunk), 1)

    def sub_body(i, c):
        r0 = i * sub
        srows = pl.ds(row0 + r0, sub)
        qb = q_ref[srows, :]
        kb = k_ref[srows, :]
        vb = v_ref[srows, :]
        bb = b_scr[pl.ds(r0, sub), :]
        base = bb[0:1, :] - la_ref[pl.ds(row0 + r0, 1), :]
        o_blk = o_scr[pl.ds(r0, sub), :]
        o_h = [o_blk[:, h * GLA_DV:(h + 1) * GLA_DV] for h in range(GLA_HEADS)]
        if r0 > 0:
            qi = qb * jnp.exp(bb - base)
            kpre = (k[:r0] * jnp.exp(base - b[:r0])).astype(BF16)
            for h in range(GLA_HEADS):
                qih = jnp.where(lane_head == h, qi, 0.0).astype(BF16)
                a = _nt(qih, kpre).astype(BF16)
                o_h[h] = o_h[h] + jnp.dot(a, v_bf[:r0, h * GLA_DV:(h + 1) * GLA_DV],
                                          preferred_element_type=F32)
        o_blk = jnp.concatenate(o_h, axis=1)
        prods = []
        for s in range(sub):
            e = jnp.exp(jnp.minimum(bb - bb[s:s + 1, :], 0.0))
            prods.append(jnp.where(t_id >= s, qb * kb[s:s + 1, :] * e, 0.0))
        pall = jnp.concatenate(prods, axis=0).astype(BF16)
        abc = jnp.dot(pall, seg, preferred_element_type=F32)
        for s in range(sub):
            o_blk = o_blk + abc[s * sub:(s + 1) * sub, :] * vb[s:s + 1, :]
        r = r_ref[srows, :]
        outs = []
        for h in range(GLA_HEADS):
            oh = o_blk[:, h * GLA_DV:(h + 1) * GLA_DV]
            ms = jnp.mean(oh * oh, axis=-1, keepdims=True)
            outs.append(oh * lax.rsqrt(ms + RMS_EPS) * g_ref[...])
        y = jnp.concatenate(outs, axis=1)
        o_ref[srows, :] = (y * (r * jax.nn.sigmoid(r))).astype(o_ref.dtype)
        return c

    for i in range(nsub):
        sub_body(i, 0)


def _gla_prompt_kernel(q_ref, k_ref, v_ref, la_ref, r_ref, g_ref, o_ref, sout_ref,
                       s_ref, b_scr, o_scr, *, chunk, sub, n_chunks):
    @pl.when(pl.program_id(1) == 0)
    def _():
        s_ref[...] = jnp.zeros_like(s_ref)

    def body(c, carry):
        _gla_chunk(q_ref, k_ref, v_ref, la_ref, r_ref, g_ref, o_ref, s_ref, b_scr, o_scr,
                   pl.multiple_of(c * chunk, chunk), chunk, sub)
        return carry

    lax.fori_loop(0, n_chunks, body, 0)
    sout_ref[...] = s_ref[...]


def _gla_prompt(gq, gk, gv, la, gr, g, batch, seq):
    step = min(GLA_STEP, seq)
    chunk = min(GLA_CHUNK, seq)
    sub = min(GLA_SUB, chunk)
    assert seq % step == 0 and step % chunk == 0 and chunk % sub == 0
    ns = seq // step
    row = lambda c: pl.BlockSpec((step, c), lambda b, i: (b * ns + i, 0))
    kern = functools.partial(_gla_prompt_kernel, chunk=chunk, sub=sub, n_chunks=step // chunk)
    return pl.pallas_call(
        kern,
        grid=(batch, ns),
        in_specs=[row(GLA_K_WIDTH), row(GLA_K_WIDTH), row(GLA_V_WIDTH), row(GLA_K_WIDTH),
                  row(GLA_V_WIDTH), _const_spec((1, GLA_DV))],
        out_specs=[row(GLA_V_WIDTH),
                   pl.BlockSpec((None, GLA_K_WIDTH, GLA_DV), lambda b, i: (b, 0, 0))],
        out_shape=[jax.ShapeDtypeStruct((batch * seq, GLA_V_WIDTH), BF16),
                   jax.ShapeDtypeStruct((batch, GLA_K_WIDTH, GLA_DV), F32)],
        scratch_shapes=[pltpu.VMEM((GLA_K_WIDTH, GLA_DV), F32),
                        pltpu.VMEM((chunk, GLA_K_WIDTH), F32),
                        pltpu.VMEM((chunk, GLA_V_WIDTH), F32)],
        compiler_params=_params(("parallel", "arbitrary")),
        name="gla_prompt",
    )(gq, gk, gv, la, gr, g)


def _gla_sample_kernel(q_ref, k_ref, v_ref, la_ref, r_ref, g_ref, s0_ref, o_ref, sout_ref,
                       s_ref, b_scr, o_scr, *, chunk):
    s_ref[...] = s0_ref[...]
    _gla_chunk(q_ref, k_ref, v_ref, la_ref, r_ref, g_ref, o_ref, s_ref, b_scr, o_scr,
               0, chunk, chunk)
    sout_ref[...] = s_ref[...]


def _gla_sample(gq, gk, gv, la, gr, g, s0, row_off, nseq, ts):
    assert row_off % ts == 0 and ts <= GLA_CHUNK and ts % SUBLANES == 0
    blk0 = row_off // ts
    row = lambda c: pl.BlockSpec((ts, c), lambda b: (blk0 + b, 0))
    st = pl.BlockSpec((None, GLA_K_WIDTH, GLA_DV), lambda b: (b, 0, 0))
    kern = functools.partial(_gla_sample_kernel, chunk=ts)
    return pl.pallas_call(
        kern,
        grid=(nseq,),
        in_specs=[row(GLA_K_WIDTH), row(GLA_K_WIDTH), row(GLA_V_WIDTH), row(GLA_K_WIDTH),
                  row(GLA_V_WIDTH), _const_spec((1, GLA_DV)), st],
        out_specs=[pl.BlockSpec((ts, GLA_V_WIDTH), lambda b: (b, 0)), st],
        out_shape=[jax.ShapeDtypeStruct((nseq * ts, GLA_V_WIDTH), BF16),
                   jax.ShapeDtypeStruct((nseq, GLA_K_WIDTH, GLA_DV), F32)],
        scratch_shapes=[pltpu.VMEM((GLA_K_WIDTH, GLA_DV), F32),
                        pltpu.VMEM((ts, GLA_K_WIDTH), F32),
                        pltpu.VMEM((ts, GLA_V_WIDTH), F32)],
        compiler_params=_params(("parallel",)),
        name="gla_sample",
    )(gq, gk, gv, la, gr, g, s0)


def _head_pair_perm():
    cols = []
    for p in range(ATTN_HEADS // ATTN_KV_HEADS):
        for g in range(ATTN_KV_HEADS):
            h = p + g * (ATTN_HEADS // ATTN_KV_HEADS)
            cols.extend(range(h * HEAD_DIM, (h + 1) * HEAD_DIM))
    return np.asarray(cols, np.int32)


def _prep_w_in(w):
    d = w.shape[0]
    splits = np.cumsum([ATTN_WIDTH, KV_WIDTH, KV_WIDTH, IDX_HEADS * IDX_DIM, IDX_DIM, IDX_HEADS,
                        GLA_K_WIDTH, GLA_K_WIDTH, GLA_V_WIDTH, GLA_V_WIDTH, GLA_GATE_RANK])[:-1]
    q, k, v, iq, ik, iw, gq, gk, gv, gr, glr = jnp.split(w, [int(s) for s in splits], axis=1)
    zeros = lambda c: jnp.zeros((d, c), w.dtype)
    parts = [q[:, _head_pair_perm()], k, v, iq, ik, ik, iw, zeros(LANES - IDX_HEADS),
             gq, gk, gv, gr, glr, zeros(LANES - GLA_GATE_RANK)]
    out = jnp.concatenate(parts, axis=1).astype(BF16)
    assert out.shape[1] == C_END
    return out


def kernel(x_prompt, x_sample, cache_k, cache_v, cache_idx_k, state_gla, page_table,
           attn_norm_g, w_in, idx_knorm_g, idx_knorm_b, gla_gate_up, gla_gate_b, gla_onorm_g,
           w_out, ffn_norm_g, w_gate, w_up, w_down, rel_bias, final_norm_g):
    B, T, D = x_prompt.shape
    DB, TS, _ = x_sample.shape
    depth = w_in.shape[0]
    n_pool, page = cache_k.shape[1], cache_k.shape[2]
    n_pages = page_table.shape[1]
    past_len = n_pages * page
    topk_p = min(IDX_TOPK_MAX, T // 4)
    topk_s = min(IDX_TOPK_MAX, (past_len + TS) // 4)
    NP, NS = B * T, DB * TS
    N = NP + NS
    assert N % TOKEN_TILE == 0 and NP % TS == 0

    tq = min(ATTN_TILE, T)
    btab = _bias_tables(rel_bias, tq, tq, offsets=(tq, 0), key_major=True)
    cfar = btab[0, :, 0, tq - 1]
    ktot = past_len + page
    bias_s = _bias_tables(rel_bias, TS, ktot, offsets=(past_len,)).reshape(ATTN_HEADS * TS, ktot)

    ck = cache_k.reshape(depth, n_pool, page, KV_WIDTH)
    cv = cache_v.reshape(depth, n_pool, page, KV_WIDTH)
    perm = _head_pair_perm()
    lane = np.arange(LANES)
    half_mask = np.stack([lane < HEAD_DIM, lane >= HEAD_DIM])

    x = jnp.concatenate([x_prompt.reshape(NP, D), x_sample.reshape(NS, D)], axis=0)
    outs = {n: [] for n in ("kp", "vp", "ikp", "sp", "ks", "vs", "iks", "ss")}
    for l in range(depth):
        w = _prep_w_in(w_in[l])
        lng = jnp.concatenate([idx_knorm_g[l], idx_knorm_g[l]])[None, :]
        lnb = jnp.concatenate([idx_knorm_b[l], idx_knorm_b[l]])[None, :]
        gup = jnp.zeros((LANES, GLA_K_WIDTH), BF16).at[:GLA_GATE_RANK].set(gla_gate_up[l].astype(BF16))
        (q, k32, v32, kb, vb, iq, ik32, ik2, iw, gq, gk, gv, gr, la) = _inproj(
            x, attn_norm_g[l][None, :], w, lng, lnb, gup, gla_gate_b[l][None, :])

        ao_p = _dsa_prompt(iq, iw[:NP, :IDX_HEADS].T, q, ik2, kb, vb[:NP].T, btab, cfar,
                             B, T, topk_p)
        onorm = gla_onorm_g[l][None, :]
        go_p, s_p = _gla_prompt(gq, gk, gv, la, gr, onorm, B, T)

        iq_s = iq[NP:].reshape(DB, TS, IDX_HEADS, IDX_DIM).transpose(0, 2, 1, 3)
        iq_s = iq_s.reshape(DB, IDX_HEADS * TS, IDX_DIM)
        iw_s = iw[NP:, :IDX_HEADS].reshape(DB, TS, IDX_HEADS).transpose(0, 2, 1)
        iw_s = jnp.broadcast_to(iw_s.reshape(DB, IDX_HEADS * TS, 1), (DB, IDX_HEADS * TS, LANES))
        q_s = q[NP:].reshape(DB, TS, ATTN_HEADS // 2, LANES)
        q_s = jnp.stack([jnp.where(half_mask[g], q_s, jnp.zeros_like(q_s))
                         for g in range(ATTN_KV_HEADS)], axis=1)
        q_s = q_s.transpose(0, 1, 3, 2, 4).reshape(DB, ATTN_HEADS * TS, LANES)
        pad_rows = lambda a: jnp.pad(a.reshape(DB, TS, a.shape[-1]), ((0, 0), (0, page - TS), (0, 0)))
        ao_s = _dsa_sample(page_table, iq_s, iw_s, q_s, pad_rows(ik32[NP:]), pad_rows(kb[NP:]),
                           pad_rows(vb[NP:]), bias_s, cache_idx_k, ck, cv, l, topk_s)
        ao_s = ao_s.reshape(DB, ATTN_HEADS // 2, TS, LANES).transpose(0, 2, 1, 3).reshape(NS, ATTN_WIDTH)
        s0 = state_gla[l].reshape(DB, GLA_K_WIDTH, GLA_DV)
        go_s, s_s = _gla_sample(gq, gk, gv, la, gr, onorm, s0, NP, DB, TS)

        ao = jnp.concatenate([ao_p, ao_s], axis=0)
        go = jnp.concatenate([go_p, go_s], axis=0)
        wo = jnp.concatenate([w_out[l][:ATTN_WIDTH][perm], w_out[l][ATTN_WIDTH:]], axis=0).astype(BF16)
        x = _ffn(x, ao, go, wo, ffn_norm_g[l][None, :], w_gate[l].astype(BF16),
                 w_up[l].astype(BF16), w_down[l].astype(BF16))

        outs["kp"].append(k32[:NP].reshape(B, T, ATTN_KV_HEADS, HEAD_DIM))
        outs["vp"].append(v32[:NP].reshape(B, T, ATTN_KV_HEADS, HEAD_DIM))
        outs["ikp"].append(ik32[:NP].reshape(B, T, IDX_DIM))
        outs["sp"].append(s_p.reshape(B, GLA_HEADS, GLA_DK, GLA_DV))
        outs["ks"].append(k32[NP:].reshape(DB, TS, ATTN_KV_HEADS, HEAD_DIM))
        outs["vs"].append(v32[NP:].reshape(DB, TS, ATTN_KV_HEADS, HEAD_DIM))
        outs["iks"].append(ik32[NP:].reshape(DB, TS, IDX_DIM))
        outs["ss"].append(s_s.reshape(DB, GLA_HEADS, GLA_DK, GLA_DV))

    y = _rmsnorm(x, final_norm_g[None, :])
    st = lambda n: jnp.stack(outs[n])
    return (y[:NP].reshape(B, T, D), y[NP:].reshape(DB, TS, D),
            st("kp"), st("vp"), st("ikp"), st("sp"), st("ks"), st("vs"), st("iks"), st("ss"))
```

```python
import functools
import math

import jax
import jax.numpy as jnp
import numpy as np
from jax import lax
from jax.experimental import pallas as pl
from jax.experimental.pallas import tpu as pltpu

F32 = jnp.float32
BF16 = jnp.bfloat16
I32 = jnp.int32

ATTN_HEADS = 8
ATTN_KV_HEADS = 2
HEAD_DIM = 64
ATTN_WIDTH = ATTN_HEADS * HEAD_DIM
KV_WIDTH = ATTN_KV_HEADS * HEAD_DIM
IDX_HEADS = 8
IDX_DIM = 64
IDX_TOPK_MAX = 256
GLA_HEADS = 4
GLA_DK = 64
GLA_DV = 128
GLA_K_WIDTH = GLA_HEADS * GLA_DK
GLA_V_WIDTH = GLA_HEADS * GLA_DV
GLA_GATE_RANK = 16
GLA_GATE_TAU = 16.0
GLA_CHUNK = 64
GLA_SUB = 16
REL_BUCKETS = 32
REL_MAX_DIST = 128
RMS_EPS = 1e-6
LN_EPS = 1e-5

LANES = 128
SUBLANES = 8
VMEM_LIMIT = 56 * 1024 * 1024

TOKEN_TILE = 512
FFN_CHUNK = 256
ATTN_TILE = 256
FAR_SPAN = 4
KEY_SLAB = 32
COARSE_SLAB = 64
GLA_STEP = 512

INT_MIN = np.int32(-2 ** 31)
INT_MAX = np.int32(2 ** 31 - 1)
NEG = -1e30
LOG2E = math.log2(math.e)
Q_SCALE = HEAD_DIM ** -0.5 * LOG2E

C_Q = 0
C_K = 512
C_V = 640
C_IQ = 768
C_IK = 1280
C_IW = 1408
C_GQ = 1536
C_GK = 1792
C_GV = 2048
C_GR = 2560
C_GLR = 3072
C_END = 3200


def _nt(a, b):
    return lax.dot_general(a, b, (((1,), (1,)), ((), ())), preferred_element_type=F32)


def _tn(a, b, precision=None):
    return lax.dot_general(a, b, (((0,), (0,)), ((), ())), preferred_element_type=F32,
                           precision=precision)


def _rep(x, n):
    return x if n == 1 else jnp.concatenate([x] * n, axis=1)


def _const_spec(shape):
    nd = len(shape)
    return pl.BlockSpec(shape, lambda *_: (0,) * nd, pipeline_mode=pl.Buffered(1))


def _params(sem):
    return pltpu.CompilerParams(dimension_semantics=sem, vmem_limit_bytes=VMEM_LIMIT)


def _inproj_kernel(x_ref, g_ref, w_ref, lng_ref, lnb_ref, gup_ref, gb_ref,
                   q_ref, k_ref, v_ref, kb_ref, vb_ref, iq_ref, ik_ref, ik2_ref, iw_ref,
                   gq_ref, gk_ref, gv_ref, gr_ref, la_ref):
    x = x_ref[...]
    ms = jnp.mean(x * x, axis=-1, keepdims=True)
    h = (x * lax.rsqrt(ms + RMS_EPS) * g_ref[...]).astype(BF16)

    def mm(a, b):
        return jnp.dot(h, w_ref[:, a:b], preferred_element_type=F32)

    q_ref[...] = (mm(C_Q, C_K) * Q_SCALE).astype(BF16)
    k = mm(C_K, C_V)
    k_ref[...] = k
    kb_ref[...] = k.astype(BF16)
    v = mm(C_V, C_IQ)
    v_ref[...] = v
    vb_ref[...] = v.astype(BF16)
    iq_ref[...] = mm(C_IQ, C_IK).astype(BF16)

    z = mm(C_IK, C_IW)
    mu = jnp.mean(z, axis=-1, keepdims=True)
    zc = z - mu
    var = jnp.mean(zc * zc, axis=-1, keepdims=True)
    ikn = zc * lax.rsqrt(var + LN_EPS) * lng_ref[...] + lnb_ref[...]
    ik_ref[...] = ikn[:, :IDX_DIM]
    ik2_ref[...] = ikn.astype(BF16)

    iw_ref[...] = mm(C_IW, C_GQ) * (IDX_HEADS ** -0.5 * IDX_DIM ** -0.5)
    gq_ref[...] = mm(C_GQ, C_GK) * (GLA_DK ** -0.5)
    gk_ref[...] = mm(C_GK, C_GV)
    gv_ref[...] = mm(C_GV, C_GR)
    gr_ref[...] = mm(C_GR, C_GLR)
    glr = mm(C_GLR, C_END).astype(BF16)
    gate = jnp.dot(glr, gup_ref[...], preferred_element_type=F32) + gb_ref[...]
    log_sig = jnp.minimum(gate, 0.0) - jnp.log1p(jnp.exp(-jnp.abs(gate)))
    la_ref[...] = log_sig / GLA_GATE_TAU


def _inproj(x, g, w, lng, lnb, gup, gb):
    n, d = x.shape
    tm = TOKEN_TILE
    row = lambda c: pl.BlockSpec((tm, c), lambda i: (i, 0))
    outs = [(ATTN_WIDTH, BF16), (KV_WIDTH, F32), (KV_WIDTH, F32), (KV_WIDTH, BF16), (KV_WIDTH, BF16),
            (IDX_HEADS * IDX_DIM, BF16), (IDX_DIM, F32), (2 * IDX_DIM, BF16), (LANES, F32),
            (GLA_K_WIDTH, F32), (GLA_K_WIDTH, F32), (GLA_V_WIDTH, F32), (GLA_V_WIDTH, F32),
            (GLA_K_WIDTH, F32)]
    return pl.pallas_call(
        _inproj_kernel,
        grid=(n // tm,),
        in_specs=[row(d), _const_spec((1, d)), _const_spec(w.shape), _const_spec(lng.shape),
                  _const_spec(lnb.shape), _const_spec(gup.shape), _const_spec(gb.shape)],
        out_specs=[row(c) for c, _ in outs],
        out_shape=[jax.ShapeDtypeStruct((n, c), dt) for c, dt in outs],
        compiler_params=_params(("parallel",)),
        name="inproj",
    )(x, g, w, lng, lnb, gup, gb)


def _ffn_kernel(x_ref, ao_ref, go_ref, wo_ref, g_ref, wg_ref, wu_ref, wd_ref, o_ref):
    half = ao_ref.shape[1]
    x1 = (x_ref[...]
          + jnp.dot(ao_ref[...], wo_ref[:half, :], preferred_element_type=F32)
          + jnp.dot(go_ref[...], wo_ref[half:, :], preferred_element_type=F32))
    ms = jnp.mean(x1 * x1, axis=-1, keepdims=True)
    h = (x1 * lax.rsqrt(ms + RMS_EPS) * g_ref[...]).astype(BF16)
    acc = None
    d_ff = wg_ref.shape[1]
    for c in range(0, d_ff, FFN_CHUNK):
        gt = jnp.dot(h, wg_ref[:, c:c + FFN_CHUNK], preferred_element_type=F32)
        up = jnp.dot(h, wu_ref[:, c:c + FFN_CHUNK], preferred_element_type=F32)
        u = (gt * jax.nn.sigmoid(gt) * up).astype(BF16)
        part = jnp.dot(u, wd_ref[c:c + FFN_CHUNK, :], preferred_element_type=F32)
        acc = part if acc is None else acc + part
    o_ref[...] = x1 + acc


def _ffn(x, ao, go, wo, g, wg, wu, wd):
    n, d = x.shape
    tm = TOKEN_TILE
    assert wg.shape[1] % FFN_CHUNK == 0
    row = lambda c: pl.BlockSpec((tm, c), lambda i: (i, 0))
    return pl.pallas_call(
        _ffn_kernel,
        grid=(n // tm,),
        in_specs=[row(d), row(ao.shape[1]), row(go.shape[1]), _const_spec(wo.shape),
                  _const_spec((1, d)), _const_spec(wg.shape), _const_spec(wu.shape),
                  _const_spec(wd.shape)],
        out_specs=row(d),
        out_shape=jax.ShapeDtypeStruct((n, d), F32),
        compiler_params=_params(("parallel",)),
        name="outproj_ffn",
    )(x, ao, go, wo, g, wg, wu, wd)


def _rmsnorm_kernel(x_ref, g_ref, o_ref):
    x = x_ref[...]
    ms = jnp.mean(x * x, axis=-1, keepdims=True)
    o_ref[...] = x * lax.rsqrt(ms + RMS_EPS) * g_ref[...]


def _rmsnorm(x, g):
    n, d = x.shape
    tm = TOKEN_TILE
    return pl.pallas_call(
        _rmsnorm_kernel,
        grid=(n // tm,),
        in_specs=[pl.BlockSpec((tm, d), lambda i: (i, 0)), _const_spec((1, d))],
        out_specs=pl.BlockSpec((tm, d), lambda i: (i, 0)),
        out_shape=jax.ShapeDtypeStruct((n, d), F32),
        compiler_params=_params(("parallel",)),
        name="final_norm",
    )(x, g)


def _rel_bucket(dist):
    n = jnp.maximum(dist, 0)
    max_exact = REL_BUCKETS // 2
    large = max_exact + (jnp.log(jnp.maximum(n, 1).astype(F32) / max_exact)
                         / math.log(REL_MAX_DIST / max_exact)
                         * (REL_BUCKETS - max_exact)).astype(I32)
    large = jnp.minimum(large, REL_BUCKETS - 1)
    return jnp.where(n < max_exact, n, large)


def _bias_kernel(rb_ref, o_ref, *, offsets, key_major):
    _, nh, r, c = o_ref.shape
    row = lax.broadcasted_iota(I32, (r, c), 0)
    col = lax.broadcasted_iota(I32, (r, c), 1)
    for t, off in enumerate(offsets):
        bucket = _rel_bucket((col - row if key_major else row - col) + off)
        for h in range(nh):
            val = jnp.zeros((r, c), F32)
            for b in range(REL_BUCKETS):
                val = jnp.where(bucket == b, rb_ref[b, h], val)
            o_ref[t, h] = val * LOG2E


def _bias_tables(rel_bias, rows, cols, offsets, key_major=False):
    nh = rel_bias.shape[1]
    return pl.pallas_call(
        functools.partial(_bias_kernel, offsets=tuple(offsets), key_major=key_major),
        in_specs=[pl.BlockSpec(memory_space=pltpu.SMEM)],
        out_shape=jax.ShapeDtypeStruct((len(offsets), nh, rows, cols), F32),
        name="bias_tables",
    )(rel_bias)


def _score_key(s):
    bits = lax.bitcast_convert_type(s, I32)
    return bits ^ ((bits >> 31) & INT_MAX)


TINY_F32 = float(np.finfo(np.float32).tiny)


def _float_key(x):
    bits = int(np.float32(x).view(np.int32))
    return np.int32(bits ^ ((bits >> 31) & 0x7FFFFFFF))


def _topk_threshold(count_ge, rows, topk):
    def step(it, base):
        shift = 30 - 2 * it
        u = base
        for digit in (1, 2, 3):
            cand = base | lax.shift_left(jnp.int32(digit), shift)
            u = jnp.where(count_ge(cand ^ INT_MIN) >= topk, cand, u)
        return u

    u = lax.fori_loop(0, 16, step, jnp.zeros((rows, LANES), I32))
    return u ^ INT_MIN


def _tie_limit(count_eq_below, need, rows, idx_bits):
    def bit_body(it, p):
        bit = lax.shift_left(jnp.int32(1), idx_bits - 1 - it)
        p2 = p | bit
        cnt = count_eq_below(p2)
        return jnp.where(cnt < need, p2, p)

    return lax.fori_loop(0, idx_bits, bit_body, jnp.zeros((rows, LANES), I32))


def _dsa_prompt_kernel(iq_ref, iwt_ref, q_ref, ik2_ref, k_ref, vt_ref, btab_ref, cfar_ref,
                         o_ref,
                         skey_ref, sbf_ref, iqm_ref, qm_ref, m_ref, l_ref, acc_ref, thr_ref, p_ref,
                         madd_ref, lg0_ref, lg1_ref, pb0_ref, pb1_ref,
                         *, tq, topk, idx_bits):
    i = pl.program_id(1)
    sl = SUBLANES
    slab = KEY_SLAB
    cslab = COARSE_SLAB
    lane = lax.broadcasted_iota(I32, (1, LANES), 1)
    lo_half = lane < HEAD_DIM
    key_id = lax.broadcasted_iota(I32, (tq, tq), 0)
    qry_id = lax.broadcasted_iota(I32, (tq, tq), 1)
    slab_id = lax.broadcasted_iota(I32, (slab, tq), 0)

    for h in range(IDX_HEADS):
        grp = iq_ref[:, LANES * (h // 2):LANES * (h // 2 + 1)]
        iqm_ref[h] = jnp.where(lo_half if h % 2 == 0 else ~lo_half, grp, jnp.zeros_like(grp))
    for h in range(ATTN_HEADS):
        p, g = h % 4, h // 4
        grp = q_ref[:, LANES * p:LANES * (p + 1)]
        qm_ref[h] = jnp.where(lo_half if g == 0 else ~lo_half, grp, jnp.zeros_like(grp))
        m_ref[h] = jnp.full((sl, tq), NEG, F32)
        l_ref[h] = jnp.zeros((sl, tq), F32)
        acc_ref[h] = jnp.zeros((KV_WIDTH, tq), F32)

    def key_rows(j, n=1):
        return pl.ds(pl.multiple_of(j * tq, tq), n * tq)

    def score_tile(j, diag):
        ikt = ik2_ref[key_rows(j), :]
        s = jnp.zeros((tq, tq), F32)
        for h in range(IDX_HEADS):
            d = _nt(ikt, iqm_ref[h])
            s = s + iwt_ref[h:h + 1, :] * jnp.maximum(d, 0.0)
        bits = lax.bitcast_convert_type(s, I32)
        key = bits ^ ((bits >> 31) & INT_MAX)
        coarse = lax.bitcast_convert_type(bits & jnp.int32(-65536), F32)
        if diag:
            key = jnp.where(key_id <= qry_id, key, INT_MIN)
            coarse = jnp.where(key_id <= qry_id, coarse, -jnp.inf)
        skey_ref[j] = key
        sbf_ref[j] = coarse.astype(BF16)

    def score_body(j, c):
        score_tile(j, False)
        return c

    lax.fori_loop(0, i, score_body, 0)
    score_tile(i, True)

    def tall(x, rows=slab):
        return jnp.concatenate([x] * (rows // sl), axis=0)

    def count_keys(pred):
        def body(j, acc):
            for r in range(tq // slab):
                acc = acc + jnp.where(pred(skey_ref[j, r * slab:(r + 1) * slab, :],
                                           j * tq + r * slab), 1.0, 0.0)
            return acc
        acc = lax.fori_loop(0, i + 1, body, jnp.zeros((slab, tq), F32))
        return jnp.broadcast_to(jnp.sum(acc, axis=0, keepdims=True), (sl, tq))

    def count_coarse(cand):
        one = jnp.ones((cslab, tq), BF16)
        zero = jnp.zeros((cslab, tq), BF16)

        def body(j, acc):
            for r in range(tq // cslab):
                acc = acc + jnp.where(sbf_ref[j, r * cslab:(r + 1) * cslab, :] >= cand, one, zero)
            return acc
        acc = lax.fori_loop(0, i + 1, body, zero)
        return jnp.broadcast_to(jnp.sum(acc.astype(F32), axis=0, keepdims=True), (sl, tq))

    def coarse_value(u):
        code = lax.shift_right_arithmetic(lax.shift_left(u ^ 0x8000, 16), 16)
        hi = code ^ (lax.shift_right_arithmetic(code, 31) & 0x7FFF)
        return lax.bitcast_convert_type(lax.shift_left(hi, 16), F32)

    def coarse_body(it, u):
        u2 = u | lax.shift_left(jnp.int32(1), 15 - it)
        cand = tall(coarse_value(u2), cslab).astype(BF16)
        return jnp.where(count_coarse(cand) >= topk, u2, u)

    u16 = lax.fori_loop(0, 16, coarse_body, jnp.zeros((sl, tq), I32))
    t16 = coarse_value(u16)
    real_thr = t16 > -jnp.inf
    tiny = jnp.abs(t16) < TINY_F32
    k32 = lax.shift_left(u16 ^ 0x8000, 16)
    lo0 = jnp.where(tiny, _float_key(-TINY_F32), k32 - 0x10000)
    hi0 = jnp.where(tiny, _float_key(TINY_F32), k32 + 0x10000)
    lo0 = jnp.where(real_thr, lo0, INT_MIN)
    hi0 = jnp.where(real_thr, hi0, INT_MIN + 1)
    c_lo0 = count_keys(lambda kt, _: kt >= tall(lo0))

    def active(lo, hi, c_lo):
        return jnp.logical_and(hi - lo > 1, c_lo != topk)

    def bisect_cond(carry):
        lo, hi, c_lo = carry
        return jnp.max(jnp.where(active(lo, hi, c_lo), 1.0, 0.0)) > 0.0

    def bisect_body(carry):
        lo, hi, c_lo = carry
        mid = lo + lax.shift_right_arithmetic(hi - lo, 1)
        mid_t = tall(mid)
        c_mid = count_keys(lambda kt, _: kt >= mid_t)
        act = active(lo, hi, c_lo)
        up = jnp.logical_and(act, c_mid >= topk)
        down = jnp.logical_and(act, c_mid < topk)
        return (jnp.where(up, mid, lo), jnp.where(down, mid, hi), jnp.where(up, c_mid, c_lo))

    thr, _, cnt_ge = lax.while_loop(bisect_cond, bisect_body, (lo0, hi0, c_lo0))
    thr_t = tall(thr)
    tie = jnp.logical_and(cnt_ge > topk, real_thr)
    thr_ref[...] = thr
    p_ref[...] = jnp.where(real_thr, INT_MAX, jnp.int32(-1))

    @pl.when(jnp.max(jnp.where(tie, 1.0, 0.0)) > 0.0)
    def _():
        need = topk - count_keys(lambda kt, _: kt > thr_t)

        def tie_body(it, p):
            p2 = p | lax.shift_left(jnp.int32(1), idx_bits - 1 - it)
            p2_t = tall(p2)
            cnt = count_keys(lambda kt, base: jnp.logical_and(kt == thr_t, base + slab_id < p2_t))
            return jnp.where(cnt < need, p2, p)

        plim = lax.fori_loop(0, idx_bits, tie_body, jnp.zeros((sl, tq), I32))
        p_ref[...] = jnp.where(tie, plim, p_ref[...])

    def attn_span(j0, nt, near):
        nkeys = nt * tq
        kb = k_ref[key_rows(j0, nt), :]
        vt = vt_ref[:, key_rows(j0, nt)]
        thr_t = tall(thr_ref[...])
        plim_t = tall(p_ref[...])

        def mask_body(r, c):
            r0 = pl.multiple_of(r * slab, slab)
            for t in range(nt):
                kt = skey_ref[j0 + t, pl.ds(r0, slab), :]
                idx = (j0 + t) * tq + r0 + slab_id
                sel = jnp.logical_or(kt > thr_t, jnp.logical_and(kt == thr_t, idx <= plim_t))
                madd_ref[pl.ds(t * tq + r0, slab), :] = jnp.where(sel, 0.0, NEG)
            return c

        lax.fori_loop(0, tq // slab, mask_body, 0)

        bufs = ((lg0_ref, pb0_ref), (lg1_ref, pb1_ref))
        chunk = LANES
        per_chunk = chunk // slab

        def logits_chunk(hh, c):
            rows = slice(c * chunk, (c + 1) * chunk)
            bufs[hh % 2][0][rows, :] = _nt(kb[rows], qm_ref[hh])

        for c in range(nkeys // chunk):
            logits_chunk(0, c)
        for h in range(ATTN_HEADS):
            lg_ref, pb_ref = bufs[h % 2]
            mx = jnp.full((slab, tq), NEG, F32)
            for r in range(nkeys // slab):
                if h + 1 < ATTN_HEADS and r % per_chunk == 0:
                    logits_chunk(h + 1, r // per_chunk)
                rows = slice(r * slab, (r + 1) * slab)
                lg = lg_ref[rows, :] + madd_ref[rows, :]
                if near is not None:
                    lg = lg + (btab_ref[near, h, rows, :] - cfar_ref[h])
                lg_ref[rows, :] = lg
                mx = jnp.maximum(mx, lg)
            m_old = m_ref[h]
            m_new = jnp.maximum(m_old, jnp.max(mx, axis=0, keepdims=True))
            alpha = jnp.exp2(m_old - m_new)
            m_ref[h] = m_new
            m_t = tall(m_new)
            ps = jnp.zeros((slab, tq), F32)
            for r in range(nkeys // slab):
                rows = slice(r * slab, (r + 1) * slab)
                pexp = jnp.exp2(lg_ref[rows, :] - m_t)
                ps = ps + pexp
                pb_ref[rows, :] = pexp.astype(BF16)
            psum = ps[0:sl]
            for r in range(1, slab // sl):
                psum = psum + ps[r * sl:(r + 1) * sl]
            l_ref[h] = alpha * l_ref[h] + psum
            acc_ref[h] = alpha[0:1, :] * acc_ref[h] + jnp.dot(vt, pb_ref[:nkeys, :],
                                                              preferred_element_type=F32)

    n_far = jnp.maximum(i - 1, 0)
    n_span = n_far // FAR_SPAN

    def far_span_body(s, c):
        attn_span(s * FAR_SPAN, FAR_SPAN, None)
        return c

    lax.fori_loop(0, n_span, far_span_body, 0)

    done = n_span * FAR_SPAN
    if FAR_SPAN >= 4:
        half = FAR_SPAN // 2
        n_half = (n_far - done) // half

        def far_half_body(s, c):
            attn_span(done + s * half, half, None)
            return c

        lax.fori_loop(0, n_half, far_half_body, 0)
        done = done + n_half * half

    def far_body(j, c):
        attn_span(j, 1, None)
        return c

    lax.fori_loop(done, n_far, far_body, 0)

    @pl.when(i >= 1)
    def _():
        attn_span(i - 1, 1, 0)

    attn_span(i, 1, 1)

    dim_id = lax.broadcasted_iota(I32, (KV_WIDTH, tq), 0)
    for p in range(4):
        o_lo = acc_ref[p] / jnp.sum(l_ref[p], axis=0, keepdims=True)
        o_hi = acc_ref[p + 4] / jnp.sum(l_ref[p + 4], axis=0, keepdims=True)
        o_t = jnp.where(dim_id < HEAD_DIM, o_lo, o_hi)
        o_ref[:, LANES * p:LANES * (p + 1)] = o_t.T.astype(o_ref.dtype)


def _dsa_prompt(iq, iwt, q, ik2, kb, vt, btab, cfar, batch, seq, topk):
    tq = min(ATTN_TILE, seq)
    assert seq % tq == 0 and tq % LANES == 0
    nq = seq // tq
    idx_bits = max(1, int(seq).bit_length())
    span = FAR_SPAN * tq
    qrow = lambda c: pl.BlockSpec((tq, c), lambda b, i: (b * nq + i, 0))
    seqblk = lambda c: pl.BlockSpec((seq, c), lambda b, i: (b, 0))
    kern = functools.partial(_dsa_prompt_kernel, tq=tq, topk=float(topk), idx_bits=idx_bits)
    return pl.pallas_call(
        kern,
        grid=(batch, nq),
        in_specs=[qrow(IDX_HEADS * IDX_DIM),
                  pl.BlockSpec((IDX_HEADS, tq), lambda b, i: (0, b * nq + i)),
                  qrow(ATTN_WIDTH), seqblk(2 * IDX_DIM), seqblk(KV_WIDTH),
                  pl.BlockSpec((KV_WIDTH, seq), lambda b, i: (0, b)),
                  _const_spec(btab.shape), pl.BlockSpec(memory_space=pltpu.SMEM)],
        out_specs=qrow(ATTN_WIDTH),
        out_shape=jax.ShapeDtypeStruct((batch * seq, ATTN_WIDTH), BF16),
        scratch_shapes=[pltpu.VMEM((nq, tq, tq), I32),
                        pltpu.VMEM((nq, tq, tq), BF16),
                        pltpu.VMEM((IDX_HEADS, tq, LANES), BF16),
                        pltpu.VMEM((ATTN_HEADS, tq, LANES), BF16),
                        pltpu.VMEM((ATTN_HEADS, SUBLANES, tq), F32),
                        pltpu.VMEM((ATTN_HEADS, SUBLANES, tq), F32),
                        pltpu.VMEM((ATTN_HEADS, KV_WIDTH, tq), F32),
                        pltpu.VMEM((SUBLANES, tq), I32),
                        pltpu.VMEM((SUBLANES, tq), I32),
                        pltpu.VMEM((span, tq), F32),
                        pltpu.VMEM((span, tq), F32),
                        pltpu.VMEM((span, tq), F32),
                        pltpu.VMEM((span, tq), BF16),
                        pltpu.VMEM((span, tq), BF16)],
        compiler_params=_params(("parallel", "arbitrary")),
        name="dsa_prompt",
    )(iq, iwt, q, ik2, kb, vt, btab, cfar)


def _dsa_sample_kernel(pt_ref, iq_ref, iwr_ref, q_ref, ikn_ref, kn_ref, vn_ref, bias_ref, *rest,
                       n_pages, page, ts, topk, idx_bits):
    ikp = rest[:n_pages]
    kp = rest[n_pages:2 * n_pages]
    vp = rest[2 * n_pages:3 * n_pages]
    o_ref = rest[3 * n_pages]
    skey_ref, lg_ref, p_ref = rest[3 * n_pages + 1:]
    del pt_ref
    nt = n_pages + 1
    lane = lax.broadcasted_iota(I32, (1, LANES), 1)
    lo_half = lane < HEAD_DIM
    iq = iq_ref[...]
    iwr = iwr_ref[...]
    qh = q_ref[...]

    def score(keys_bf):
        d = _nt(iq, keys_bf)
        w = iwr * jnp.maximum(d, 0.0)
        s = w[0:ts]
        for h in range(1, IDX_HEADS):
            s = s + w[h * ts:(h + 1) * ts]
        return _score_key(s)

    for pg in range(n_pages):
        skey_ref[:, pg * page:(pg + 1) * page] = score(ikp[pg][...].astype(BF16))
    new_key = score(ikn_ref[...].astype(BF16))
    qrow = lax.broadcasted_iota(I32, (ts, page), 0)
    qcol = lax.broadcasted_iota(I32, (ts, page), 1)
    skey_ref[:, n_pages * page:] = jnp.where(qcol <= qrow, new_key, INT_MIN)

    def count_tiles(pred):
        acc = jnp.zeros((ts, LANES), F32)
        for c in range(nt):
            acc = acc + jnp.where(pred(skey_ref[:, c * LANES:(c + 1) * LANES], c * LANES), 1.0, 0.0)
        return jnp.sum(acc, axis=1, keepdims=True)

    thr = _topk_threshold(lambda cand: count_tiles(lambda kt, _: kt >= cand), ts, topk)
    cnt_ge = count_tiles(lambda kt, _: kt >= thr)
    real_thr = thr[:, :1] > INT_MIN
    tie = jnp.logical_and(cnt_ge > topk, real_thr)
    p_ref[...] = jnp.broadcast_to(jnp.where(real_thr, INT_MAX, jnp.int32(-1)), (ts, LANES))

    @pl.when(jnp.max(jnp.where(tie, 1.0, 0.0)) > 0.0)
    def _():
        need = topk - count_tiles(lambda kt, _: kt > thr)

        def count_eq_below(p):
            return count_tiles(lambda kt, base: jnp.logical_and(kt == thr, base + lane < p))

        p_ref[...] = jnp.where(tie, _tie_limit(count_eq_below, need, ts, idx_bits), p_ref[...])

    plim = p_ref[...]

    ktot = nt * page
    for pg in range(n_pages):
        lg_ref[:, pg * page:(pg + 1) * page] = _nt(qh, kp[pg][...].astype(BF16))
    lg_ref[:, n_pages * page:] = _nt(qh, kn_ref[...])
    kt = skey_ref[...]
    thr_full = _rep(thr, ktot // LANES)
    idx = lax.broadcasted_iota(I32, (ts, ktot), 1)
    sel = jnp.logical_or(kt > thr_full,
                         jnp.logical_and(kt == thr_full, idx <= _rep(plim, ktot // LANES)))
    sel = jnp.concatenate([sel.astype(F32)] * ATTN_HEADS, axis=0) > 0.5
    lg = lg_ref[...] + bias_ref[...]
    lg = jnp.where(sel, lg, NEG)
    m = jnp.max(lg, axis=1, keepdims=True)
    pexp = jnp.exp2(lg - m)
    denom = jnp.sum(pexp, axis=1, keepdims=True)
    pb = pexp.astype(BF16)
    acc = jnp.dot(pb[:, n_pages * page:], vn_ref[...], preferred_element_type=F32)
    for pg in range(n_pages):
        acc = acc + jnp.dot(pb[:, pg * page:(pg + 1) * page], vp[pg][...].astype(BF16),
                            preferred_element_type=F32)
    o = acc / denom
    half = (ATTN_HEADS // 2) * ts
    o_ref[...] = jnp.where(lo_half, o[:half], o[half:]).astype(o_ref.dtype)


def _dsa_sample(page_table, iq, iwr, qh, ikn, kn, vn, bias, cidx, ck, cv, layer, topk):
    nseq, n_pages = page_table.shape
    page = cidx.shape[2]
    ts = iq.shape[1] // IDX_HEADS
    assert page == LANES
    idx_bits = int((n_pages + 1) * page).bit_length()
    seq3 = lambda a: pl.BlockSpec((None,) + a.shape[1:], lambda b, pt: (b, 0, 0))

    def page_spec(a, pg):
        return pl.BlockSpec((None, None) + a.shape[2:], lambda b, pt: (layer, pt[b, pg], 0, 0))

    kern = functools.partial(_dsa_sample_kernel, n_pages=n_pages, page=page, ts=ts,
                             topk=float(topk), idx_bits=idx_bits)
    ktot = (n_pages + 1) * page
    in_specs = ([seq3(iq), seq3(iwr), seq3(qh), seq3(ikn), seq3(kn), seq3(vn),
                 pl.BlockSpec(bias.shape, lambda b, pt: (0, 0))]
                + [page_spec(cidx, pg) for pg in range(n_pages)]
                + [page_spec(ck, pg) for pg in range(n_pages)]
                + [page_spec(cv, pg) for pg in range(n_pages)])
    rows_out = (ATTN_HEADS // 2) * ts
    return pl.pallas_call(
        kern,
        grid_spec=pltpu.PrefetchScalarGridSpec(
            num_scalar_prefetch=1,
            grid=(nseq,),
            in_specs=in_specs,
            out_specs=pl.BlockSpec((None, rows_out, LANES), lambda b, pt: (b, 0, 0)),
            scratch_shapes=[pltpu.VMEM((ts, ktot), I32),
                            pltpu.VMEM((ATTN_HEADS * ts, ktot), F32),
                            pltpu.VMEM((ts, LANES), I32)]),
        out_shape=jax.ShapeDtypeStruct((nseq, rows_out, LANES), BF16),
        compiler_params=_params(("arbitrary",)),
        name="dsa_sample",
    )(page_table, iq, iwr, qh, ikn, kn, vn, bias,
      *([cidx] * n_pages), *([ck] * n_pages), *([cv] * n_pages))


def _gla_chunk(q_ref, k_ref, v_ref, la_ref, r_ref, g_ref, o_ref, s_ref, b_scr, o_scr,
               row0, chunk, sub):
    kw = GLA_K_WIDTH
    rows = pl.ds(row0, chunk)
    q = q_ref[rows, :]
    k = k_ref[rows, :]
    v = v_ref[rows, :]
    la = la_ref[rows, :]
    hi = lax.Precision.HIGHEST
    tri = (lax.broadcasted_iota(I32, (chunk, chunk), 0)
           >= lax.broadcasted_iota(I32, (chunk, chunk), 1)).astype(F32)
    b = jnp.dot(tri, la, preferred_element_type=F32, precision=hi)
    b_scr[...] = b
    lane_head = lax.broadcasted_iota(I32, (1, kw), 1) // GLA_DK
    row_head = lax.broadcasted_iota(I32, (kw, GLA_DV), 0) // GLA_DK
    s_old = s_ref[...]
    s_bf = s_old.astype(BF16)

    qe = q * jnp.exp(b)
    b_last_col = _tn(la, jnp.ones((chunk, GLA_DV), F32), precision=hi)
    b_last = b[chunk - 1:chunk, :]
    ke = (k * jnp.exp(b_last - b)).astype(BF16)
    s_new = jnp.exp(b_last_col) * s_old
    for h in range(GLA_HEADS):
        qh = jnp.where(lane_head == h, qe, 0.0).astype(BF16)
        o_scr[:, h * GLA_DV:(h + 1) * GLA_DV] = jnp.dot(qh, s_bf, preferred_element_type=F32)
        kv = _tn(ke, v[:, h * GLA_DV:(h + 1) * GLA_DV].astype(BF16))
        s_new = s_new + jnp.where(row_head == h, kv, 0.0)
    s_ref[...] = s_new

    nsub = chunk // sub
    seg = (lax.broadcasted_iota(I32, (kw, GLA_V_WIDTH), 0) // GLA_DK
           == lax.broadcasted_iota(I32, (kw, GLA_V_WIDTH), 1) // GLA_DV).astype(BF16)
    t_id = lax.broadcasted_iota(I32, (sub, kw), 0)
    v_bf = v.astype(BF16)
    col = lax.broadcasted_iota(I32, (sub, chunk), 1)

    def sub_body(i, c):
        r0 = i * sub
        srows = pl.ds(row0 + r0, sub)
        qb = q_ref[srows, :]
        kb = k_ref[srows, :]
        vb = v_ref[srows, :]
        bb = b_scr[pl.ds(r0, sub), :]
        base = bb[0:1, :] - la_ref[pl.ds(row0 + r0, 1), :]
        o_blk = o_scr[pl.ds(r0, sub), :]
        o_h = [o_blk[:, h * GLA_DV:(h + 1) * GLA_DV] for h in range(GLA_HEADS)]
        if r0 > 0:
            qi = qb * jnp.exp(bb - base)
            kpre = (k[:r0] * jnp.exp(base - b[:r0])).astype(BF16)
            for h in range(GLA_HEADS):
                qih = jnp.where(lane_head == h, qi, 0.0).astype(BF16)
                a = _nt(qih, kpre).astype(BF16)
                o_h[h] = o_h[h] + jnp.dot(a, v_bf[:r0, h * GLA_DV:(h + 1) * GLA_DV],
                                          preferred_element_type=F32)
        o_blk = jnp.concatenate(o_h, axis=1)
        prods = []
        for s in range(sub):
            e = jnp.exp(jnp.minimum(bb - bb[s:s + 1, :], 0.0))
            prods.append(jnp.where(t_id >= s, qb * kb[s:s + 1, :] * e, 0.0))
        pall = jnp.concatenate(prods, axis=0).astype(BF16)
        abc = jnp.dot(pall, seg, preferred_element_type=F32)
        for s in range(sub):
            o_blk = o_blk + abc[s * sub:(s + 1) * sub, :] * vb[s:s + 1, :]
        r = r_ref[srows, :]
        outs = []
        for h in range(GLA_HEADS):
            oh = o_blk[:, h * GLA_DV:(h + 1) * GLA_DV]
            ms = jnp.mean(oh * oh, axis=-1, keepdims=True)
            outs.append(oh * lax.rsqrt(ms + RMS_EPS) * g_ref[...])
        y = jnp.concatenate(outs, axis=1)
        o_ref[srows, :] = (y * (r * jax.nn.sigmoid(r))).astype(o_ref.dtype)
        return c

    for i in range(nsub):
        sub_body(i, 0)


def _gla_prompt_kernel(q_ref, k_ref, v_ref, la_ref, r_ref, g_ref, o_ref, sout_ref,
                       s_ref, b_scr, o_scr, *, chunk, sub, n_chunks):
    @pl.when(pl.program_id(1) == 0)
    def _():
        s_ref[...] = jnp.zeros_like(s_ref)

    def body(c, carry):
        _gla_chunk(q_ref, k_ref, v_ref, la_ref, r_ref, g_ref, o_ref, s_ref, b_scr, o_scr,
                   pl.multiple_of(c * chunk, chunk), chunk, sub)
        return carry

    lax.fori_loop(0, n_chunks, body, 0)
    sout_ref[...] = s_ref[...]


def _gla_prompt(gq, gk, gv, la, gr, g, batch, seq):
    step = min(GLA_STEP, seq)
    chunk = min(GLA_CHUNK, seq)
    sub = min(GLA_SUB, chunk)
    assert seq % step == 0 and step % chunk == 0 and chunk % sub == 0
    ns = seq // step
    row = lambda c: pl.BlockSpec((step, c), lambda b, i: (b * ns + i, 0))
    kern = functools.partial(_gla_prompt_kernel, chunk=chunk, sub=sub, n_chunks=step // chunk)
    return pl.pallas_call(
        kern,
        grid=(batch, ns),
        in_specs=[row(GLA_K_WIDTH), row(GLA_K_WIDTH), row(GLA_V_WIDTH), row(GLA_K_WIDTH),
                  row(GLA_V_WIDTH), _const_spec((1, GLA_DV))],
        out_specs=[row(GLA_V_WIDTH),
                   pl.BlockSpec((None, GLA_K_WIDTH, GLA_DV), lambda b, i: (b, 0, 0))],
        out_shape=[jax.ShapeDtypeStruct((batch * seq, GLA_V_WIDTH), BF16),
                   jax.ShapeDtypeStruct((batch, GLA_K_WIDTH, GLA_DV), F32)],
        scratch_shapes=[pltpu.VMEM((GLA_K_WIDTH, GLA_DV), F32),
                        pltpu.VMEM((chunk, GLA_K_WIDTH), F32),
                        pltpu.VMEM((chunk, GLA_V_WIDTH), F32)],
        compiler_params=_params(("parallel", "arbitrary")),
        name="gla_prompt",
    )(gq, gk, gv, la, gr, g)


def _gla_sample_kernel(q_ref, k_ref, v_ref, la_ref, r_ref, g_ref, s0_ref, o_ref, sout_ref,
                       s_ref, b_scr, o_scr, *, chunk):
    s_ref[...] = s0_ref[...]
    _gla_chunk(q_ref, k_ref, v_ref, la_ref, r_ref, g_ref, o_ref, s_ref, b_scr, o_scr,
               0, chunk, chunk)
    sout_ref[...] = s_ref[...]


def _gla_sample(gq, gk, gv, la, gr, g, s0, row_off, nseq, ts):
    assert row_off % ts == 0 and ts <= GLA_CHUNK and ts % SUBLANES == 0
    blk0 = row_off // ts
    row = lambda c: pl.BlockSpec((ts, c), lambda b: (blk0 + b, 0))
    st = pl.BlockSpec((None, GLA_K_WIDTH, GLA_DV), lambda b: (b, 0, 0))
    kern = functools.partial(_gla_sample_kernel, chunk=ts)
    return pl.pallas_call(
        kern,
        grid=(nseq,),
        in_specs=[row(GLA_K_WIDTH), row(GLA_K_WIDTH), row(GLA_V_WIDTH), row(GLA_K_WIDTH),
                  row(GLA_V_WIDTH), _const_spec((1, GLA_DV)), st],
        out_specs=[pl.BlockSpec((ts, GLA_V_WIDTH), lambda b: (b, 0)), st],
        out_shape=[jax.ShapeDtypeStruct((nseq * ts, GLA_V_WIDTH), BF16),
                   jax.ShapeDtypeStruct((nseq, GLA_K_WIDTH, GLA_DV), F32)],
        scratch_shapes=[pltpu.VMEM((GLA_K_WIDTH, GLA_DV), F32),
                        pltpu.VMEM((ts, GLA_K_WIDTH), F32),
                        pltpu.VMEM((ts, GLA_V_WIDTH), F32)],
        compiler_params=_params(("parallel",)),
        name="gla_sample",
    )(gq, gk, gv, la, gr, g, s0)


def _head_pair_perm():
    cols = []
    for p in range(ATTN_HEADS // ATTN_KV_HEADS):
        for g in range(ATTN_KV_HEADS):
            h = p + g * (ATTN_HEADS // ATTN_KV_HEADS)
            cols.extend(range(h * HEAD_DIM, (h + 1) * HEAD_DIM))
    return np.asarray(cols, np.int32)


def _prep_w_in(w):
    d = w.shape[0]
    splits = np.cumsum([ATTN_WIDTH, KV_WIDTH, KV_WIDTH, IDX_HEADS * IDX_DIM, IDX_DIM, IDX_HEADS,
                        GLA_K_WIDTH, GLA_K_WIDTH, GLA_V_WIDTH, GLA_V_WIDTH, GLA_GATE_RANK])[:-1]
    q, k, v, iq, ik, iw, gq, gk, gv, gr, glr = jnp.split(w, [int(s) for s in splits], axis=1)
    zeros = lambda c: jnp.zeros((d, c), w.dtype)
    parts = [q[:, _head_pair_perm()], k, v, iq, ik, ik, iw, zeros(LANES - IDX_HEADS),
             gq, gk, gv, gr, glr, zeros(LANES - GLA_GATE_RANK)]
    out = jnp.concatenate(parts, axis=1).astype(BF16)
    assert out.shape[1] == C_END
    return out


def kernel(x_prompt, x_sample, cache_k, cache_v, cache_idx_k, state_gla, page_table,
           attn_norm_g, w_in, idx_knorm_g, idx_knorm_b, gla_gate_up, gla_gate_b, gla_onorm_g,
           w_out, ffn_norm_g, w_gate, w_up, w_down, rel_bias, final_norm_g):
    B, T, D = x_prompt.shape
    DB, TS, _ = x_sample.shape
    depth = w_in.shape[0]
    n_pool, page = cache_k.shape[1], cache_k.shape[2]
    n_pages = page_table.shape[1]
    past_len = n_pages * page
    topk_p = min(IDX_TOPK_MAX, T // 4)
    topk_s = min(IDX_TOPK_MAX, (past_len + TS) // 4)
    NP, NS = B * T, DB * TS
    assert NP % TOKEN_TILE == 0 and NS % TOKEN_TILE == 0

    tq = min(ATTN_TILE, T)
    btab = _bias_tables(rel_bias, tq, tq, offsets=(tq, 0), key_major=True)
    cfar = btab[0, :, 0, tq - 1]
    ktot = past_len + page
    bias_s = _bias_tables(rel_bias, TS, ktot, offsets=(past_len,)).reshape(ATTN_HEADS * TS, ktot)

    ck = cache_k.reshape(depth, n_pool, page, KV_WIDTH)
    cv = cache_v.reshape(depth, n_pool, page, KV_WIDTH)
    perm = _head_pair_perm()
    lane = np.arange(LANES)
    half_mask = np.stack([lane < HEAD_DIM, lane >= HEAD_DIM])

    xp = x_prompt.reshape(NP, D)
    xs = x_sample.reshape(NS, D)
    outs = {n: [] for n in ("kp", "vp", "ikp", "sp", "ks", "vs", "iks", "ss")}
    pad_rows = lambda a: jnp.pad(a.reshape(DB, TS, a.shape[-1]), ((0, 0), (0, page - TS), (0, 0)))
    for l in range(depth):
        w = _prep_w_in(w_in[l])
        lng = jnp.concatenate([idx_knorm_g[l], idx_knorm_g[l]])[None, :]
        lnb = jnp.concatenate([idx_knorm_b[l], idx_knorm_b[l]])[None, :]
        gup = jnp.zeros((LANES, GLA_K_WIDTH), BF16).at[:GLA_GATE_RANK].set(gla_gate_up[l].astype(BF16))
        proj = lambda x: _inproj(x, attn_norm_g[l][None, :], w, lng, lnb, gup,
                                 gla_gate_b[l][None, :])
        onorm = gla_onorm_g[l][None, :]
        wo = jnp.concatenate([w_out[l][:ATTN_WIDTH][perm], w_out[l][ATTN_WIDTH:]], axis=0).astype(BF16)
        ffn = lambda x, ao, go: _ffn(x, ao, go, wo, ffn_norm_g[l][None, :], w_gate[l].astype(BF16),
                                     w_up[l].astype(BF16), w_down[l].astype(BF16))

        (q, k32, v32, kb, vb, iq, ik32, ik2, iw, gq, gk, gv, gr, la) = proj(xp)
        ao = _dsa_prompt(iq, iw[:, :IDX_HEADS].T, q, ik2, kb, vb.T, btab, cfar, B, T, topk_p)
        go, s_p = _gla_prompt(gq, gk, gv, la, gr, onorm, B, T)
        xp = ffn(xp, ao, go)
        outs["kp"].append(k32.reshape(B, T, ATTN_KV_HEADS, HEAD_DIM))
        outs["vp"].append(v32.reshape(B, T, ATTN_KV_HEADS, HEAD_DIM))
        outs["ikp"].append(ik32.reshape(B, T, IDX_DIM))
        outs["sp"].append(s_p.reshape(B, GLA_HEADS, GLA_DK, GLA_DV))

        (q, k32, v32, kb, vb, iq, ik32, ik2, iw, gq, gk, gv, gr, la) = proj(xs)
        iq_s = iq.reshape(DB, TS, IDX_HEADS, IDX_DIM).transpose(0, 2, 1, 3)
        iq_s = iq_s.reshape(DB, IDX_HEADS * TS, IDX_DIM)
        iw_s = iw[:, :IDX_HEADS].reshape(DB, TS, IDX_HEADS).transpose(0, 2, 1)
        iw_s = jnp.broadcast_to(iw_s.reshape(DB, IDX_HEADS * TS, 1), (DB, IDX_HEADS * TS, LANES))
        q_s = q.reshape(DB, TS, ATTN_HEADS // 2, LANES)
        q_s = jnp.stack([jnp.where(half_mask[g], q_s, jnp.zeros_like(q_s))
                         for g in range(ATTN_KV_HEADS)], axis=1)
        q_s = q_s.transpose(0, 1, 3, 2, 4).reshape(DB, ATTN_HEADS * TS, LANES)
        ao = _dsa_sample(page_table, iq_s, iw_s, q_s, pad_rows(ik32), pad_rows(kb), pad_rows(vb),
                         bias_s, cache_idx_k, ck, cv, l, topk_s)
        ao = ao.reshape(DB, ATTN_HEADS // 2, TS, LANES).transpose(0, 2, 1, 3).reshape(NS, ATTN_WIDTH)
        s0 = state_gla[l].reshape(DB, GLA_K_WIDTH, GLA_DV)
        go, s_s = _gla_sample(gq, gk, gv, la, gr, onorm, s0, 0, DB, TS)
        xs = ffn(xs, ao, go)
        outs["ks"].append(k32.reshape(DB, TS, ATTN_KV_HEADS, HEAD_DIM))
        outs["vs"].append(v32.reshape(DB, TS, ATTN_KV_HEADS, HEAD_DIM))
        outs["iks"].append(ik32.reshape(DB, TS, IDX_DIM))
        outs["ss"].append(s_s.reshape(DB, GLA_HEADS, GLA_DK, GLA_DV))

    yp = _rmsnorm(xp, final_norm_g[None, :]).reshape(B, T, D)
    ys = _rmsnorm(xs, final_norm_g[None, :]).reshape(DB, TS, D)
    st = lambda n: jnp.stack(outs[n])
    return (yp, ys, st("kp"), st("vp"), st("ikp"), st("sp"), st("ks"), st("vs"), st("iks"), st("ss"))
```

```python
import functools
import math

import jax
import jax.numpy as jnp
import numpy as np
from jax import lax
from jax.experimental import pallas as pl
from jax.experimental.pallas import tpu as pltpu

F32 = jnp.float32
BF16 = jnp.bfloat16
I32 = jnp.int32

ATTN_HEADS = 8
ATTN_KV_HEADS = 2
HEAD_DIM = 64
ATTN_WIDTH = ATTN_HEADS * HEAD_DIM
KV_WIDTH = ATTN_KV_HEADS * HEAD_DIM
IDX_HEADS = 8
IDX_DIM = 64
IDX_TOPK_MAX = 256
GLA_HEADS = 4
GLA_DK = 64
GLA_DV = 128
GLA_K_WIDTH = GLA_HEADS * GLA_DK
GLA_V_WIDTH = GLA_HEADS * GLA_DV
GLA_GATE_RANK = 16
GLA_GATE_TAU = 16.0
GLA_CHUNK = 64
GLA_SUB = 16
REL_BUCKETS = 32
REL_MAX_DIST = 128
RMS_EPS = 1e-6
LN_EPS = 1e-5

LANES = 128
SUBLANES = 8
VMEM_LIMIT = 56 * 1024 * 1024

TOKEN_TILE = 512
FFN_CHUNK = 256
ATTN_TILE = 256
FAR_SPAN = 4
KEY_SLAB = 32
COARSE_SLAB = 64
GLA_STEP = 512

INT_MIN = np.int32(-2 ** 31)
INT_MAX = np.int32(2 ** 31 - 1)
NEG = -1e30
LOG2E = math.log2(math.e)
Q_SCALE = HEAD_DIM ** -0.5 * LOG2E

C_Q = 0
C_K = 512
C_V = 640
C_IQ = 768
C_IK = 1280
C_IW = 1408
C_GQ = 1536
C_GK = 1792
C_GV = 2048
C_GR = 2560
C_GLR = 3072
C_END = 3200


def _nt(a, b):
    return lax.dot_general(a, b, (((1,), (1,)), ((), ())), preferred_element_type=F32)


def _tn(a, b, precision=None):
    return lax.dot_general(a, b, (((0,), (0,)), ((), ())), preferred_element_type=F32,
                           precision=precision)


def _rep(x, n):
    return x if n == 1 else jnp.concatenate([x] * n, axis=1)


def _const_spec(shape):
    nd = len(shape)
    return pl.BlockSpec(shape, lambda *_: (0,) * nd, pipeline_mode=pl.Buffered(1))


def _params(sem):
    return pltpu.CompilerParams(dimension_semantics=sem, vmem_limit_bytes=VMEM_LIMIT)


def _inproj_kernel(x_ref, g_ref, w_ref, lng_ref, lnb_ref, gup_ref, gb_ref,
                   q_ref, k_ref, v_ref, kb_ref, vb_ref, iq_ref, ik_ref, ik2_ref, iw_ref,
                   gq_ref, gk_ref, gv_ref, gr_ref, la_ref):
    x = x_ref[...]
    ms = jnp.mean(x * x, axis=-1, keepdims=True)
    h = (x * lax.rsqrt(ms + RMS_EPS) * g_ref[...]).astype(BF16)

    def mm(a, b):
        return jnp.dot(h, w_ref[:, a:b], preferred_element_type=F32)

    q_ref[...] = (mm(C_Q, C_K) * Q_SCALE).astype(BF16)
    k = mm(C_K, C_V)
    k_ref[...] = k
    kb_ref[...] = k.astype(BF16)
    v = mm(C_V, C_IQ)
    v_ref[...] = v
    vb_ref[...] = v.astype(BF16)
    iq_ref[...] = mm(C_IQ, C_IK).astype(BF16)

    z = mm(C_IK, C_IW)
    mu = jnp.mean(z, axis=-1, keepdims=True)
    zc = z - mu
    var = jnp.mean(zc * zc, axis=-1, keepdims=True)
    ikn = zc * lax.rsqrt(var + LN_EPS) * lng_ref[...] + lnb_ref[...]
    ik_ref[...] = ikn[:, :IDX_DIM]
    ik2_ref[...] = ikn.astype(BF16)

    iw_ref[...] = mm(C_IW, C_GQ) * (IDX_HEADS ** -0.5 * IDX_DIM ** -0.5)
    gq_ref[...] = mm(C_GQ, C_GK) * (GLA_DK ** -0.5)
    gk_ref[...] = mm(C_GK, C_GV)
    gv_ref[...] = mm(C_GV, C_GR)
    gr_ref[...] = mm(C_GR, C_GLR)
    glr = mm(C_GLR, C_END).astype(BF16)
    gate = jnp.dot(glr, gup_ref[...], preferred_element_type=F32) + gb_ref[...]
    log_sig = jnp.minimum(gate, 0.0) - jnp.log1p(jnp.exp(-jnp.abs(gate)))
    la_ref[...] = log_sig / GLA_GATE_TAU


def _inproj(x, g, w, lng, lnb, gup, gb):
    n, d = x.shape
    tm = TOKEN_TILE
    row = lambda c: pl.BlockSpec((tm, c), lambda i: (i, 0))
    outs = [(ATTN_WIDTH, BF16), (KV_WIDTH, F32), (KV_WIDTH, F32), (KV_WIDTH, BF16), (KV_WIDTH, BF16),
            (IDX_HEADS * IDX_DIM, BF16), (IDX_DIM, F32), (2 * IDX_DIM, BF16), (LANES, F32),
            (GLA_K_WIDTH, F32), (GLA_K_WIDTH, F32), (GLA_V_WIDTH, F32), (GLA_V_WIDTH, F32),
            (GLA_K_WIDTH, F32)]
    return pl.pallas_call(
        _inproj_kernel,
        grid=(n // tm,),
        in_specs=[row(d), _const_spec((1, d)), _const_spec(w.shape), _const_spec(lng.shape),
                  _const_spec(lnb.shape), _const_spec(gup.shape), _const_spec(gb.shape)],
        out_specs=[row(c) for c, _ in outs],
        out_shape=[jax.ShapeDtypeStruct((n, c), dt) for c, dt in outs],
        compiler_params=_params(("parallel",)),
        name="inproj",
    )(x, g, w, lng, lnb, gup, gb)


def _ffn_kernel(x_ref, ao_ref, go_ref, wo_ref, g_ref, wg_ref, wu_ref, wd_ref, o_ref):
    half = ao_ref.shape[1]
    x1 = (x_ref[...]
          + jnp.dot(ao_ref[...], wo_ref[:half, :], preferred_element_type=F32)
          + jnp.dot(go_ref[...], wo_ref[half:, :], preferred_element_type=F32))
    ms = jnp.mean(x1 * x1, axis=-1, keepdims=True)
    h = (x1 * lax.rsqrt(ms + RMS_EPS) * g_ref[...]).astype(BF16)
    acc = None
    d_ff = wg_ref.shape[1]
    for c in range(0, d_ff, FFN_CHUNK):
        gt = jnp.dot(h, wg_ref[:, c:c + FFN_CHUNK], preferred_element_type=F32)
        up = jnp.dot(h, wu_ref[:, c:c + FFN_CHUNK], preferred_element_type=F32)
        u = (gt * jax.nn.sigmoid(gt) * up).astype(BF16)
        part = jnp.dot(u, wd_ref[c:c + FFN_CHUNK, :], preferred_element_type=F32)
        acc = part if acc is None else acc + part
    o_ref[...] = x1 + acc


def _ffn(x, ao, go, wo, g, wg, wu, wd):
    n, d = x.shape
    tm = TOKEN_TILE
    assert wg.shape[1] % FFN_CHUNK == 0
    row = lambda c: pl.BlockSpec((tm, c), lambda i: (i, 0))
    return pl.pallas_call(
        _ffn_kernel,
        grid=(n // tm,),
        in_specs=[row(d), row(ao.shape[1]), row(go.shape[1]), _const_spec(wo.shape),
                  _const_spec((1, d)), _const_spec(wg.shape), _const_spec(wu.shape),
                  _const_spec(wd.shape)],
        out_specs=row(d),
        out_shape=jax.ShapeDtypeStruct((n, d), F32),
        compiler_params=_params(("parallel",)),
        name="outproj_ffn",
    )(x, ao, go, wo, g, wg, wu, wd)


def _rmsnorm_kernel(x_ref, g_ref, o_ref):
    x = x_ref[...]
    ms = jnp.mean(x * x, axis=-1, keepdims=True)
    o_ref[...] = x * lax.rsqrt(ms + RMS_EPS) * g_ref[...]


def _rmsnorm(x, g):
    n, d = x.shape
    tm = TOKEN_TILE
    return pl.pallas_call(
        _rmsnorm_kernel,
        grid=(n // tm,),
        in_specs=[pl.BlockSpec((tm, d), lambda i: (i, 0)), _const_spec((1, d))],
        out_specs=pl.BlockSpec((tm, d), lambda i: (i, 0)),
        out_shape=jax.ShapeDtypeStruct((n, d), F32),
        compiler_params=_params(("parallel",)),
        name="final_norm",
    )(x, g)


def _rel_bucket(dist):
    n = jnp.maximum(dist, 0)
    max_exact = REL_BUCKETS // 2
    large = max_exact + (jnp.log(jnp.maximum(n, 1).astype(F32) / max_exact)
                         / math.log(REL_MAX_DIST / max_exact)
                         * (REL_BUCKETS - max_exact)).astype(I32)
    large = jnp.minimum(large, REL_BUCKETS - 1)
    return jnp.where(n < max_exact, n, large)


def _bias_kernel(rb_ref, o_ref, *, offsets, key_major):
    _, nh, r, c = o_ref.shape
    row = lax.broadcasted_iota(I32, (r, c), 0)
    col = lax.broadcasted_iota(I32, (r, c), 1)
    for t, off in enumerate(offsets):
        bucket = _rel_bucket((col - row if key_major else row - col) + off)
        for h in range(nh):
            val = jnp.zeros((r, c), F32)
            for b in range(REL_BUCKETS):
                val = jnp.where(bucket == b, rb_ref[b, h], val)
            o_ref[t, h] = val * LOG2E


def _bias_tables(rel_bias, rows, cols, offsets, key_major=False):
    nh = rel_bias.shape[1]
    return pl.pallas_call(
        functools.partial(_bias_kernel, offsets=tuple(offsets), key_major=key_major),
        in_specs=[pl.BlockSpec(memory_space=pltpu.SMEM)],
        out_shape=jax.ShapeDtypeStruct((len(offsets), nh, rows, cols), F32),
        name="bias_tables",
    )(rel_bias)


def _score_key(s):
    bits = lax.bitcast_convert_type(s, I32)
    return bits ^ ((bits >> 31) & INT_MAX)


TINY_F32 = float(np.finfo(np.float32).tiny)


def _float_key(x):
    bits = int(np.float32(x).view(np.int32))
    return np.int32(bits ^ ((bits >> 31) & 0x7FFFFFFF))


def _topk_threshold(count_ge, rows, topk):
    def step(it, base):
        shift = 30 - 2 * it
        u = base
        for digit in (1, 2, 3):
            cand = base | lax.shift_left(jnp.int32(digit), shift)
            u = jnp.where(count_ge(cand ^ INT_MIN) >= topk, cand, u)
        return u

    u = lax.fori_loop(0, 16, step, jnp.zeros((rows, LANES), I32))
    return u ^ INT_MIN


def _tie_limit(count_eq_below, need, rows, idx_bits):
    def bit_body(it, p):
        bit = lax.shift_left(jnp.int32(1), idx_bits - 1 - it)
        p2 = p | bit
        cnt = count_eq_below(p2)
        return jnp.where(cnt < need, p2, p)

    return lax.fori_loop(0, idx_bits, bit_body, jnp.zeros((rows, LANES), I32))


def _dsa_prompt_kernel(iq_ref, iwt_ref, q_ref, ik2_ref, k_ref, vt_ref, btab_ref, cfar_ref,
                         o_ref,
                         skey_ref, sbf_ref, iqm_ref, qm_ref, m_ref, l_ref, acc_ref, thr_ref, p_ref,
                         madd_ref, lg0_ref, lg1_ref, pb0_ref, pb1_ref,
                         *, tq, topk, idx_bits):
    i = pl.program_id(1)
    sl = SUBLANES
    slab = KEY_SLAB
    cslab = COARSE_SLAB
    lane = lax.broadcasted_iota(I32, (1, LANES), 1)
    lo_half = lane < HEAD_DIM
    key_id = lax.broadcasted_iota(I32, (tq, tq), 0)
    qry_id = lax.broadcasted_iota(I32, (tq, tq), 1)
    slab_id = lax.broadcasted_iota(I32, (slab, tq), 0)

    for h in range(IDX_HEADS):
        grp = iq_ref[:, LANES * (h // 2):LANES * (h // 2 + 1)]
        iqm_ref[h] = jnp.where(lo_half if h % 2 == 0 else ~lo_half, grp, jnp.zeros_like(grp))
    for h in range(ATTN_HEADS):
        p, g = h % 4, h // 4
        grp = q_ref[:, LANES * p:LANES * (p + 1)]
        qm_ref[h] = jnp.where(lo_half if g == 0 else ~lo_half, grp, jnp.zeros_like(grp))
        m_ref[h] = jnp.full((sl, tq), NEG, F32)
        l_ref[h] = jnp.zeros((sl, tq), F32)
        acc_ref[h] = jnp.zeros((KV_WIDTH, tq), F32)

    def key_rows(j, n=1):
        return pl.ds(pl.multiple_of(j * tq, tq), n * tq)

    def score_tile(j, diag):
        ikt = ik2_ref[key_rows(j), :]
        s = jnp.zeros((tq, tq), F32)
        for h in range(IDX_HEADS):
            d = _nt(ikt, iqm_ref[h])
            s = s + iwt_ref[h:h + 1, :] * jnp.maximum(d, 0.0)
        bits = lax.bitcast_convert_type(s, I32)
        key = bits ^ ((bits >> 31) & INT_MAX)
        coarse = lax.bitcast_convert_type(bits & jnp.int32(-65536), F32)
        if diag:
            key = jnp.where(key_id <= qry_id, key, INT_MIN)
            coarse = jnp.where(key_id <= qry_id, coarse, -jnp.inf)
        skey_ref[j] = key
        sbf_ref[j] = coarse.astype(BF16)

    def score_body(j, c):
        score_tile(j, False)
        return c

    lax.fori_loop(0, i, score_body, 0)
    score_tile(i, True)

    def tall(x, rows=slab):
        return jnp.concatenate([x] * (rows // sl), axis=0)

    def count_keys(pred):
        def body(j, acc):
            for r in range(tq // slab):
                acc = acc + jnp.where(pred(skey_ref[j, r * slab:(r + 1) * slab, :],
                                           j * tq + r * slab), 1.0, 0.0)
            return acc
        acc = lax.fori_loop(0, i + 1, body, jnp.zeros((slab, tq), F32))
        return jnp.broadcast_to(jnp.sum(acc, axis=0, keepdims=True), (sl, tq))

    def count_coarse(cand):
        one = jnp.ones((cslab, tq), BF16)
        zero = jnp.zeros((cslab, tq), BF16)

        def body(j, acc):
            for r in range(tq // cslab):
                acc = acc + jnp.where(sbf_ref[j, r * cslab:(r + 1) * cslab, :] >= cand, one, zero)
            return acc
        acc = lax.fori_loop(0, i + 1, body, zero)
        return jnp.broadcast_to(jnp.sum(acc.astype(F32), axis=0, keepdims=True), (sl, tq))

    def coarse_value(u):
        code = lax.shift_right_arithmetic(lax.shift_left(u ^ 0x8000, 16), 16)
        hi = code ^ (lax.shift_right_arithmetic(code, 31) & 0x7FFF)
        return lax.bitcast_convert_type(lax.shift_left(hi, 16), F32)

    def coarse_body(it, u):
        u2 = u | lax.shift_left(jnp.int32(1), 15 - it)
        cand = tall(coarse_value(u2), cslab).astype(BF16)
        return jnp.where(count_coarse(cand) >= topk, u2, u)

    u16 = lax.fori_loop(0, 16, coarse_body, jnp.zeros((sl, tq), I32))
    t16 = coarse_value(u16)
    real_thr = t16 > -jnp.inf
    tiny = jnp.abs(t16) < TINY_F32
    k32 = lax.shift_left(u16 ^ 0x8000, 16)
    lo0 = jnp.where(tiny, _float_key(-TINY_F32), k32 - 0x10000)
    hi0 = jnp.where(tiny, _float_key(TINY_F32), k32 + 0x10000)
    lo0 = jnp.where(real_thr, lo0, INT_MIN)
    hi0 = jnp.where(real_thr, hi0, INT_MIN + 1)
    c_lo0 = count_keys(lambda kt, _: kt >= tall(lo0))

    def active(lo, hi, c_lo):
        return jnp.logical_and(hi - lo > 1, c_lo != topk)

    def bisect_cond(carry):
        lo, hi, c_lo = carry
        return jnp.max(jnp.where(active(lo, hi, c_lo), 1.0, 0.0)) > 0.0

    def bisect_body(carry):
        lo, hi, c_lo = carry
        mid = lo + lax.shift_right_arithmetic(hi - lo, 1)
        mid_t = tall(mid)
        c_mid = count_keys(lambda kt, _: kt >= mid_t)
        act = active(lo, hi, c_lo)
        up = jnp.logical_and(act, c_mid >= topk)
        down = jnp.logical_and(act, c_mid < topk)
        return (jnp.where(up, mid, lo), jnp.where(down, mid, hi), jnp.where(up, c_mid, c_lo))

    thr, _, cnt_ge = lax.while_loop(bisect_cond, bisect_body, (lo0, hi0, c_lo0))
    thr_t = tall(thr)
    tie = jnp.logical_and(cnt_ge > topk, real_thr)
    thr_ref[...] = thr
    p_ref[...] = jnp.where(real_thr, INT_MAX, jnp.int32(-1))

    @pl.when(jnp.max(jnp.where(tie, 1.0, 0.0)) > 0.0)
    def _():
        need = topk - count_keys(lambda kt, _: kt > thr_t)

        def tie_body(it, p):
            p2 = p | lax.shift_left(jnp.int32(1), idx_bits - 1 - it)
            p2_t = tall(p2)
            cnt = count_keys(lambda kt, base: jnp.logical_and(kt == thr_t, base + slab_id < p2_t))
            return jnp.where(cnt < need, p2, p)

        plim = lax.fori_loop(0, idx_bits, tie_body, jnp.zeros((sl, tq), I32))
        p_ref[...] = jnp.where(tie, plim, p_ref[...])

    def attn_span(j0, nt, near):
        nkeys = nt * tq
        kb = k_ref[key_rows(j0, nt), :]
        vt = vt_ref[:, key_rows(j0, nt)]
        thr_t = tall(thr_ref[...])
        plim_t = tall(p_ref[...])

        def mask_body(r, c):
            r0 = pl.multiple_of(r * slab, slab)
            for t in range(nt):
                kt = skey_ref[j0 + t, pl.ds(r0, slab), :]
                idx = (j0 + t) * tq + r0 + slab_id
                sel = jnp.logical_or(kt > thr_t, jnp.logical_and(kt == thr_t, idx <= plim_t))
                madd_ref[pl.ds(t * tq + r0, slab), :] = jnp.where(sel, 0.0, NEG)
            return c

        lax.fori_loop(0, tq // slab, mask_body, 0)

        bufs = ((lg0_ref, pb0_ref), (lg1_ref, pb1_ref))
        chunk = LANES
        per_chunk = chunk // slab

        def logits_chunk(hh, c):
            rows = slice(c * chunk, (c + 1) * chunk)
            bufs[hh % 2][0][rows, :] = _nt(kb[rows], qm_ref[hh])

        for c in range(nkeys // chunk):
            logits_chunk(0, c)
        for h in range(ATTN_HEADS):
            lg_ref, pb_ref = bufs[h % 2]
            mx = jnp.full((slab, tq), NEG, F32)
            for r in range(nkeys // slab):
                if h + 1 < ATTN_HEADS and r % per_chunk == 0:
                    logits_chunk(h + 1, r // per_chunk)
                rows = slice(r * slab, (r + 1) * slab)
                lg = lg_ref[rows, :] + madd_ref[rows, :]
                if near is not None:
                    lg = lg + (btab_ref[near, h, rows, :] - cfar_ref[h])
                lg_ref[rows, :] = lg
                mx = jnp.maximum(mx, lg)
            m_old = m_ref[h]
            m_new = jnp.maximum(m_old, jnp.max(mx, axis=0, keepdims=True))
            alpha = jnp.exp2(m_old - m_new)
            m_ref[h] = m_new
            m_t = tall(m_new)
            ps = jnp.zeros((slab, tq), F32)
            for r in range(nkeys // slab):
                rows = slice(r * slab, (r + 1) * slab)
                pexp = jnp.exp2(lg_ref[rows, :] - m_t)
                ps = ps + pexp
                pb_ref[rows, :] = pexp.astype(BF16)
            psum = ps[0:sl]
            for r in range(1, slab // sl):
                psum = psum + ps[r * sl:(r + 1) * sl]
            l_ref[h] = alpha * l_ref[h] + psum
            acc_ref[h] = alpha[0:1, :] * acc_ref[h] + jnp.dot(vt, pb_ref[:nkeys, :],
                                                              preferred_element_type=F32)

    n_far = jnp.maximum(i - 1, 0)
    n_span = n_far // FAR_SPAN

    def far_span_body(s, c):
        attn_span(s * FAR_SPAN, FAR_SPAN, None)
        return c

    lax.fori_loop(0, n_span, far_span_body, 0)

    done = n_span * FAR_SPAN
    if FAR_SPAN >= 4:
        half = FAR_SPAN // 2
        n_half = (n_far - done) // half

        def far_half_body(s, c):
            attn_span(done + s * half, half, None)
            return c

        lax.fori_loop(0, n_half, far_half_body, 0)
        done = done + n_half * half

    def far_body(j, c):
        attn_span(j, 1, None)
        return c

    lax.fori_loop(done, n_far, far_body, 0)

    @pl.when(i >= 1)
    def _():
        attn_span(i - 1, 1, 0)

    attn_span(i, 1, 1)

    dim_id = lax.broadcasted_iota(I32, (KV_WIDTH, tq), 0)
    for p in range(4):
        o_lo = acc_ref[p] / jnp.sum(l_ref[p], axis=0, keepdims=True)
        o_hi = acc_ref[p + 4] / jnp.sum(l_ref[p + 4], axis=0, keepdims=True)
        o_t = jnp.where(dim_id < HEAD_DIM, o_lo, o_hi)
        o_ref[:, LANES * p:LANES * (p + 1)] = o_t.T.astype(o_ref.dtype)


def _dsa_prompt(iq, iwt, q, ik2, kb, vt, btab, cfar, batch, seq, topk):
    tq = min(ATTN_TILE, seq)
    assert seq % tq == 0 and tq % LANES == 0
    nq = seq // tq
    idx_bits = max(1, int(seq).bit_length())
    span = FAR_SPAN * tq
    qrow = lambda c: pl.BlockSpec((tq, c), lambda b, i: (b * nq + i, 0))
    seqblk = lambda c: pl.BlockSpec((seq, c), lambda b, i: (b, 0))
    kern = functools.partial(_dsa_prompt_kernel, tq=tq, topk=float(topk), idx_bits=idx_bits)
    return pl.pallas_call(
        kern,
        grid=(batch, nq),
        in_specs=[qrow(IDX_HEADS * IDX_DIM),
                  pl.BlockSpec((IDX_HEADS, tq), lambda b, i: (0, b * nq + i)),
                  qrow(ATTN_WIDTH), seqblk(2 * IDX_DIM), seqblk(KV_WIDTH),
                  pl.BlockSpec((KV_WIDTH, seq), lambda b, i: (0, b)),
                  _const_spec(btab.shape), pl.BlockSpec(memory_space=pltpu.SMEM)],
        out_specs=qrow(ATTN_WIDTH),
        out_shape=jax.ShapeDtypeStruct((batch * seq, ATTN_WIDTH), BF16),
        scratch_shapes=[pltpu.VMEM((nq, tq, tq), I32),
                        pltpu.VMEM((nq, tq, tq), BF16),
                        pltpu.VMEM((IDX_HEADS, tq, LANES), BF16),
                        pltpu.VMEM((ATTN_HEADS, tq, LANES), BF16),
                        pltpu.VMEM((ATTN_HEADS, SUBLANES, tq), F32),
                        pltpu.VMEM((ATTN_HEADS, SUBLANES, tq), F32),
                        pltpu.VMEM((ATTN_HEADS, KV_WIDTH, tq), F32),
                        pltpu.VMEM((SUBLANES, tq), I32),
                        pltpu.VMEM((SUBLANES, tq), I32),
                        pltpu.VMEM((span, tq), F32),
                        pltpu.VMEM((span, tq), F32),
                        pltpu.VMEM((span, tq), F32),
                        pltpu.VMEM((span, tq), BF16),
                        pltpu.VMEM((span, tq), BF16)],
        compiler_params=_params(("parallel", "arbitrary")),
        name="dsa_prompt",
    )(iq, iwt, q, ik2, kb, vt, btab, cfar)


def _dsa_sample_kernel(pt_ref, iq_ref, iwr_ref, q_ref, ikn_ref, kn_ref, vn_ref, bias_ref, *rest,
                       n_pages, page, ts, topk, idx_bits):
    ikp = rest[:n_pages]
    kp = rest[n_pages:2 * n_pages]
    vp = rest[2 * n_pages:3 * n_pages]
    o_ref = rest[3 * n_pages]
    skey_ref, lg_ref, p_ref = rest[3 * n_pages + 1:]
    del pt_ref
    nt = n_pages + 1
    lane = lax.broadcasted_iota(I32, (1, LANES), 1)
    lo_half = lane < HEAD_DIM
    iq = iq_ref[...]
    iwr = iwr_ref[...]
    qh = q_ref[...]

    def score(keys_t):
        d = jnp.dot(iq, keys_t, preferred_element_type=F32)
        w = iwr * jnp.maximum(d, 0.0)
        s = w[0:ts]
        for h in range(1, IDX_HEADS):
            s = s + w[h * ts:(h + 1) * ts]
        return _score_key(s)

    for pg in range(n_pages):
        skey_ref[:, pg * page:(pg + 1) * page] = score(ikp[pg][...].astype(BF16))
    new_key = score(ikn_ref[...].astype(BF16))
    qrow = lax.broadcasted_iota(I32, (ts, page), 0)
    qcol = lax.broadcasted_iota(I32, (ts, page), 1)
    skey_ref[:, n_pages * page:] = jnp.where(qcol <= qrow, new_key, INT_MIN)

    def count_tiles(pred):
        acc = jnp.zeros((ts, LANES), F32)
        for c in range(nt):
            acc = acc + jnp.where(pred(skey_ref[:, c * LANES:(c + 1) * LANES], c * LANES), 1.0, 0.0)
        return jnp.sum(acc, axis=1, keepdims=True)

    thr = _topk_threshold(lambda cand: count_tiles(lambda kt, _: kt >= cand), ts, topk)
    cnt_ge = count_tiles(lambda kt, _: kt >= thr)
    real_thr = thr[:, :1] > INT_MIN
    tie = jnp.logical_and(cnt_ge > topk, real_thr)
    p_ref[...] = jnp.broadcast_to(jnp.where(real_thr, INT_MAX, jnp.int32(-1)), (ts, LANES))

    @pl.when(jnp.max(jnp.where(tie, 1.0, 0.0)) > 0.0)
    def _():
        need = topk - count_tiles(lambda kt, _: kt > thr)

        def count_eq_below(p):
            return count_tiles(lambda kt, base: jnp.logical_and(kt == thr, base + lane < p))

        p_ref[...] = jnp.where(tie, _tie_limit(count_eq_below, need, ts, idx_bits), p_ref[...])

    plim = p_ref[...]

    ktot = nt * page
    for pg in range(n_pages):
        lg_ref[:, pg * page:(pg + 1) * page] = jnp.dot(qh, kp[pg][...].astype(BF16),
                                                       preferred_element_type=F32)
    lg_ref[:, n_pages * page:] = jnp.dot(qh, kn_ref[...], preferred_element_type=F32)
    kt = skey_ref[...]
    thr_full = _rep(thr, ktot // LANES)
    idx = lax.broadcasted_iota(I32, (ts, ktot), 1)
    sel = jnp.logical_or(kt > thr_full,
                         jnp.logical_and(kt == thr_full, idx <= _rep(plim, ktot // LANES)))
    sel = jnp.concatenate([sel.astype(F32)] * ATTN_HEADS, axis=0) > 0.5
    lg = lg_ref[...] + bias_ref[...]
    lg = jnp.where(sel, lg, NEG)
    m = jnp.max(lg, axis=1, keepdims=True)
    pexp = jnp.exp2(lg - m)
    denom = jnp.sum(pexp, axis=1, keepdims=True)
    pb = pexp.astype(BF16)
    acc = _nt(pb[:, n_pages * page:], vn_ref[...])
    for pg in range(n_pages):
        acc = acc + _nt(pb[:, pg * page:(pg + 1) * page], vp[pg][...].astype(BF16))
    o = acc / denom
    half = (ATTN_HEADS // 2) * ts
    o_ref[...] = jnp.where(lo_half, o[:half], o[half:]).astype(o_ref.dtype)


def _dsa_sample(page_table, iq, iwr, qh, ikn, kn, vn, bias, cidx, ck, cv, layer, topk):
    nseq, n_pages = page_table.shape
    page = cidx.shape[3]
    ts = iq.shape[1] // IDX_HEADS
    assert page == LANES
    idx_bits = int((n_pages + 1) * page).bit_length()
    seq3 = lambda a: pl.BlockSpec((None,) + a.shape[1:], lambda b, pt: (b, 0, 0))

    def page_spec(a, pg):
        return pl.BlockSpec((None, None) + a.shape[2:], lambda b, pt: (layer, pt[b, pg], 0, 0))

    kern = functools.partial(_dsa_sample_kernel, n_pages=n_pages, page=page, ts=ts,
                             topk=float(topk), idx_bits=idx_bits)
    ktot = (n_pages + 1) * page
    in_specs = ([seq3(iq), seq3(iwr), seq3(qh), seq3(ikn), seq3(kn), seq3(vn),
                 pl.BlockSpec(bias.shape, lambda b, pt: (0, 0))]
                + [page_spec(cidx, pg) for pg in range(n_pages)]
                + [page_spec(ck, pg) for pg in range(n_pages)]
                + [page_spec(cv, pg) for pg in range(n_pages)])
    rows_out = (ATTN_HEADS // 2) * ts
    return pl.pallas_call(
        kern,
        grid_spec=pltpu.PrefetchScalarGridSpec(
            num_scalar_prefetch=1,
            grid=(nseq,),
            in_specs=in_specs,
            out_specs=pl.BlockSpec((None, rows_out, LANES), lambda b, pt: (b, 0, 0)),
            scratch_shapes=[pltpu.VMEM((ts, ktot), I32),
                            pltpu.VMEM((ATTN_HEADS * ts, ktot), F32),
                            pltpu.VMEM((ts, LANES), I32)]),
        out_shape=jax.ShapeDtypeStruct((nseq, rows_out, LANES), BF16),
        compiler_params=_params(("arbitrary",)),
        name="dsa_sample",
    )(page_table, iq, iwr, qh, ikn, kn, vn, bias,
      *([cidx] * n_pages), *([ck] * n_pages), *([cv] * n_pages))


def _gla_chunk(q_ref, k_ref, v_ref, la_ref, r_ref, g_ref, o_ref, s_ref, b_scr, o_scr,
               row0, chunk, sub):
    kw = GLA_K_WIDTH
    rows = pl.ds(row0, chunk)
    q = q_ref[rows, :]
    k = k_ref[rows, :]
    v = v_ref[rows, :]
    la = la_ref[rows, :]
    hi = lax.Precision.HIGHEST
    tri = (lax.broadcasted_iota(I32, (chunk, chunk), 0)
           >= lax.broadcasted_iota(I32, (chunk, chunk), 1)).astype(F32)
    b = jnp.dot(tri, la, preferred_element_type=F32, precision=hi)
    b_scr[...] = b
    lane_head = lax.broadcasted_iota(I32, (1, kw), 1) // GLA_DK
    row_head = lax.broadcasted_iota(I32, (kw, GLA_DV), 0) // GLA_DK
    s_old = s_ref[...]
    s_bf = s_old.astype(BF16)

    qe = q * jnp.exp(b)
    b_last_col = _tn(la, jnp.ones((chunk, GLA_DV), F32), precision=hi)
    b_last = b[chunk - 1:chunk, :]
    ke = (k * jnp.exp(b_last - b)).astype(BF16)
    s_new = jnp.exp(b_last_col) * s_old
    for h in range(GLA_HEADS):
        qh = jnp.where(lane_head == h, qe, 0.0).astype(BF16)
        o_scr[:, h * GLA_DV:(h + 1) * GLA_DV] = jnp.dot(qh, s_bf, preferred_element_type=F32)
        kv = _tn(ke, v[:, h * GLA_DV:(h + 1) * GLA_DV].astype(BF16))
        s_new = s_new + jnp.where(row_head == h, kv, 0.0)
    s_ref[...] = s_new

    nsub = chunk // sub
    seg = (lax.broadcasted_iota(I32, (kw, GLA_V_WIDTH), 0) // GLA_DK
           == lax.broadcasted_iota(I32, (kw, GLA_V_WIDTH), 1) // GLA_DV).astype(BF16)
    t_id = lax.broadcasted_iota(I32, (sub, kw), 0)
    v_bf = v.astype(BF16)
    col = lax.broadcasted_iota(I32, (sub, chunk), 1)

    def sub_body(i, c):
        r0 = i * sub
        srows = pl.ds(row0 + r0, sub)
        qb = q_ref[srows, :]
        kb = k_ref[srows, :]
        vb = v_ref[srows, :]
        bb = b_scr[pl.ds(r0, sub), :]
        base = bb[0:1, :] - la_ref[pl.ds(row0 + r0, 1), :]
        o_blk = o_scr[pl.ds(r0, sub), :]
        o_h = [o_blk[:, h * GLA_DV:(h + 1) * GLA_DV] for h in range(GLA_HEADS)]
        if r0 > 0:
            qi = qb * jnp.exp(bb - base)
            kpre = (k[:r0] * jnp.exp(base - b[:r0])).astype(BF16)
            for h in range(GLA_HEADS):
                qih = jnp.where(lane_head == h, qi, 0.0).astype(BF16)
                a = _nt(qih, kpre).astype(BF16)
                o_h[h] = o_h[h] + jnp.dot(a, v_bf[:r0, h * GLA_DV:(h + 1) * GLA_DV],
                                          preferred_element_type=F32)
        o_blk = jnp.concatenate(o_h, axis=1)
        prods = []
        for s in range(sub):
            e = jnp.exp(jnp.minimum(bb - bb[s:s + 1, :], 0.0))
            prods.append(jnp.where(t_id >= s, qb * kb[s:s + 1, :] * e, 0.0))
        pall = jnp.concatenate(prods, axis=0).astype(BF16)
        abc = jnp.dot(pall, seg, preferred_element_type=F32)
        for s in range(sub):
            o_blk = o_blk + abc[s * sub:(s + 1) * sub, :] * vb[s:s + 1, :]
        r = r_ref[srows, :]
        outs = []
        for h in range(GLA_HEADS):
            oh = o_blk[:, h * GLA_DV:(h + 1) * GLA_DV]
            ms = jnp.mean(oh * oh, axis=-1, keepdims=True)
            outs.append(oh * lax.rsqrt(ms + RMS_EPS) * g_ref[...])
        y = jnp.concatenate(outs, axis=1)
        o_ref[srows, :] = (y * (r * jax.nn.sigmoid(r))).astype(o_ref.dtype)
        return c

    for i in range(nsub):
        sub_body(i, 0)


def _gla_prompt_kernel(q_ref, k_ref, v_ref, la_ref, r_ref, g_ref, o_ref, sout_ref,
                       s_ref, b_scr, o_scr, *, chunk, sub, n_chunks):
    @pl.when(pl.program_id(1) == 0)
    def _():
        s_ref[...] = jnp.zeros_like(s_ref)

    def body(c, carry):
        _gla_chunk(q_ref, k_ref, v_ref, la_ref, r_ref, g_ref, o_ref, s_ref, b_scr, o_scr,
                   pl.multiple_of(c * chunk, chunk), chunk, sub)
        return carry

    lax.fori_loop(0, n_chunks, body, 0)
    sout_ref[...] = s_ref[...]


def _gla_prompt(gq, gk, gv, la, gr, g, batch, seq):
    step = min(GLA_STEP, seq)
    chunk = min(GLA_CHUNK, seq)
    sub = min(GLA_SUB, chunk)
    assert seq % step == 0 and step % chunk == 0 and chunk % sub == 0
    ns = seq // step
    row = lambda c: pl.BlockSpec((step, c), lambda b, i: (b * ns + i, 0))
    kern = functools.partial(_gla_prompt_kernel, chunk=chunk, sub=sub, n_chunks=step // chunk)
    return pl.pallas_call(
        kern,
        grid=(batch, ns),
        in_specs=[row(GLA_K_WIDTH), row(GLA_K_WIDTH), row(GLA_V_WIDTH), row(GLA_K_WIDTH),
                  row(GLA_V_WIDTH), _const_spec((1, GLA_DV))],
        out_specs=[row(GLA_V_WIDTH),
                   pl.BlockSpec((None, GLA_K_WIDTH, GLA_DV), lambda b, i: (b, 0, 0))],
        out_shape=[jax.ShapeDtypeStruct((batch * seq, GLA_V_WIDTH), BF16),
                   jax.ShapeDtypeStruct((batch, GLA_K_WIDTH, GLA_DV), F32)],
        scratch_shapes=[pltpu.VMEM((GLA_K_WIDTH, GLA_DV), F32),
                        pltpu.VMEM((chunk, GLA_K_WIDTH), F32),
                        pltpu.VMEM((chunk, GLA_V_WIDTH), F32)],
        compiler_params=_params(("parallel", "arbitrary")),
        name="gla_prompt",
    )(gq, gk, gv, la, gr, g)


def _gla_sample_kernel(q_ref, k_ref, v_ref, la_ref, r_ref, g_ref, s0_ref, o_ref, sout_ref,
                       s_ref, b_scr, o_scr, *, chunk):
    s_ref[...] = s0_ref[...]
    _gla_chunk(q_ref, k_ref, v_ref, la_ref, r_ref, g_ref, o_ref, s_ref, b_scr, o_scr,
               0, chunk, chunk)
    sout_ref[...] = s_ref[...]


def _gla_sample(gq, gk, gv, la, gr, g, s0, row_off, nseq, ts):
    assert row_off % ts == 0 and ts <= GLA_CHUNK and ts % SUBLANES == 0
    blk0 = row_off // ts
    row = lambda c: pl.BlockSpec((ts, c), lambda b: (blk0 + b, 0))
    st = pl.BlockSpec((None, GLA_K_WIDTH, GLA_DV), lambda b: (b, 0, 0))
    kern = functools.partial(_gla_sample_kernel, chunk=ts)
    return pl.pallas_call(
        kern,
        grid=(nseq,),
        in_specs=[row(GLA_K_WIDTH), row(GLA_K_WIDTH), row(GLA_V_WIDTH), row(GLA_K_WIDTH),
                  row(GLA_V_WIDTH), _const_spec((1, GLA_DV)), st],
        out_specs=[pl.BlockSpec((ts, GLA_V_WIDTH), lambda b: (b, 0)), st],
        out_shape=[jax.ShapeDtypeStruct((nseq * ts, GLA_V_WIDTH), BF16),
                   jax.ShapeDtypeStruct((nseq, GLA_K_WIDTH, GLA_DV), F32)],
        scratch_shapes=[pltpu.VMEM((GLA_K_WIDTH, GLA_DV), F32),
                        pltpu.VMEM((ts, GLA_K_WIDTH), F32),
                        pltpu.VMEM((ts, GLA_V_WIDTH), F32)],
        compiler_params=_params(("parallel",)),
        name="gla_sample",
    )(gq, gk, gv, la, gr, g, s0)


def _head_pair_perm():
    cols = []
    for p in range(ATTN_HEADS // ATTN_KV_HEADS):
        for g in range(ATTN_KV_HEADS):
            h = p + g * (ATTN_HEADS // ATTN_KV_HEADS)
            cols.extend(range(h * HEAD_DIM, (h + 1) * HEAD_DIM))
    return np.asarray(cols, np.int32)


def _prep_w_in(w):
    d = w.shape[0]
    splits = np.cumsum([ATTN_WIDTH, KV_WIDTH, KV_WIDTH, IDX_HEADS * IDX_DIM, IDX_DIM, IDX_HEADS,
                        GLA_K_WIDTH, GLA_K_WIDTH, GLA_V_WIDTH, GLA_V_WIDTH, GLA_GATE_RANK])[:-1]
    q, k, v, iq, ik, iw, gq, gk, gv, gr, glr = jnp.split(w, [int(s) for s in splits], axis=1)
    zeros = lambda c: jnp.zeros((d, c), w.dtype)
    parts = [q[:, _head_pair_perm()], k, v, iq, ik, ik, iw, zeros(LANES - IDX_HEADS),
             gq, gk, gv, gr, glr, zeros(LANES - GLA_GATE_RANK)]
    out = jnp.concatenate(parts, axis=1).astype(BF16)
    assert out.shape[1] == C_END
    return out


def kernel(x_prompt, x_sample, cache_k, cache_v, cache_idx_k, state_gla, page_table,
           attn_norm_g, w_in, idx_knorm_g, idx_knorm_b, gla_gate_up, gla_gate_b, gla_onorm_g,
           w_out, ffn_norm_g, w_gate, w_up, w_down, rel_bias, final_norm_g):
    B, T, D = x_prompt.shape
    DB, TS, _ = x_sample.shape
    depth = w_in.shape[0]
    n_pool, page = cache_k.shape[1], cache_k.shape[2]
    n_pages = page_table.shape[1]
    past_len = n_pages * page
    topk_p = min(IDX_TOPK_MAX, T // 4)
    topk_s = min(IDX_TOPK_MAX, (past_len + TS) // 4)
    NP, NS = B * T, DB * TS
    assert NP % TOKEN_TILE == 0 and NS % TOKEN_TILE == 0

    tq = min(ATTN_TILE, T)
    btab = _bias_tables(rel_bias, tq, tq, offsets=(tq, 0), key_major=True)
    cfar = btab[0, :, 0, tq - 1]
    ktot = past_len + page
    bias_s = _bias_tables(rel_bias, TS, ktot, offsets=(past_len,)).reshape(ATTN_HEADS * TS, ktot)

    ck = cache_k.transpose(0, 1, 3, 4, 2).reshape(depth, n_pool, KV_WIDTH, page)
    cv = cache_v.transpose(0, 1, 3, 4, 2).reshape(depth, n_pool, KV_WIDTH, page)
    cidx = cache_idx_k.transpose(0, 1, 3, 2)
    perm = _head_pair_perm()
    lane = np.arange(LANES)
    half_mask = np.stack([lane < HEAD_DIM, lane >= HEAD_DIM])

    xp = x_prompt.reshape(NP, D)
    xs = x_sample.reshape(NS, D)
    outs = {n: [] for n in ("kp", "vp", "ikp", "sp", "ks", "vs", "iks", "ss")}
    new_page = lambda a: jnp.pad(a.reshape(DB, TS, a.shape[-1]),
                                 ((0, 0), (0, page - TS), (0, 0))).transpose(0, 2, 1)
    for l in range(depth):
        w = _prep_w_in(w_in[l])
        lng = jnp.concatenate([idx_knorm_g[l], idx_knorm_g[l]])[None, :]
        lnb = jnp.concatenate([idx_knorm_b[l], idx_knorm_b[l]])[None, :]
        gup = jnp.zeros((LANES, GLA_K_WIDTH), BF16).at[:GLA_GATE_RANK].set(gla_gate_up[l].astype(BF16))
        proj = lambda x: _inproj(x, attn_norm_g[l][None, :], w, lng, lnb, gup,
                                 gla_gate_b[l][None, :])
        onorm = gla_onorm_g[l][None, :]
        wo = jnp.concatenate([w_out[l][:ATTN_WIDTH][perm], w_out[l][ATTN_WIDTH:]], axis=0).astype(BF16)
        ffn = lambda x, ao, go: _ffn(x, ao, go, wo, ffn_norm_g[l][None, :], w_gate[l].astype(BF16),
                                     w_up[l].astype(BF16), w_down[l].astype(BF16))

        (q, k32, v32, kb, vb, iq, ik32, ik2, iw, gq, gk, gv, gr, la) = proj(xp)
        ao = _dsa_prompt(iq, iw[:, :IDX_HEADS].T, q, ik2, kb, vb.T, btab, cfar, B, T, topk_p)
        go, s_p = _gla_prompt(gq, gk, gv, la, gr, onorm, B, T)
        xp = ffn(xp, ao, go)
        outs["kp"].append(k32.reshape(B, T, ATTN_KV_HEADS, HEAD_DIM))
        outs["vp"].append(v32.reshape(B, T, ATTN_KV_HEADS, HEAD_DIM))
        outs["ikp"].append(ik32.reshape(B, T, IDX_DIM))
        outs["sp"].append(s_p.reshape(B, GLA_HEADS, GLA_DK, GLA_DV))

        (q, k32, v32, kb, vb, iq, ik32, ik2, iw, gq, gk, gv, gr, la) = proj(xs)
        iq_s = iq.reshape(DB, TS, IDX_HEADS, IDX_DIM).transpose(0, 2, 1, 3)
        iq_s = iq_s.reshape(DB, IDX_HEADS * TS, IDX_DIM)
        iw_s = iw[:, :IDX_HEADS].reshape(DB, TS, IDX_HEADS).transpose(0, 2, 1)
        iw_s = jnp.broadcast_to(iw_s.reshape(DB, IDX_HEADS * TS, 1), (DB, IDX_HEADS * TS, LANES))
        q_s = q.reshape(DB, TS, ATTN_HEADS // 2, LANES)
        q_s = jnp.stack([jnp.where(half_mask[g], q_s, jnp.zeros_like(q_s))
                         for g in range(ATTN_KV_HEADS)], axis=1)
        q_s = q_s.transpose(0, 1, 3, 2, 4).reshape(DB, ATTN_HEADS * TS, LANES)
        ao = _dsa_sample(page_table, iq_s, iw_s, q_s, new_page(ik32.astype(BF16)), new_page(kb),
                         new_page(vb), bias_s, cidx, ck, cv, l, topk_s)
        ao = ao.reshape(DB, ATTN_HEADS // 2, TS, LANES).transpose(0, 2, 1, 3).reshape(NS, ATTN_WIDTH)
        s0 = state_gla[l].reshape(DB, GLA_K_WIDTH, GLA_DV)
        go, s_s = _gla_sample(gq, gk, gv, la, gr, onorm, s0, 0, DB, TS)
        xs = ffn(xs, ao, go)
        outs["ks"].append(k32.reshape(DB, TS, ATTN_KV_HEADS, HEAD_DIM))
        outs["vs"].append(v32.reshape(DB, TS, ATTN_KV_HEADS, HEAD_DIM))
        outs["iks"].append(ik32.reshape(DB, TS, IDX_DIM))
        outs["ss"].append(s_s.reshape(DB, GLA_HEADS, GLA_DK, GLA_DV))

    yp = _rmsnorm(xp, final_norm_g[None, :]).reshape(B, T, D)
    ys = _rmsnorm(xs, final_norm_g[None, :]).reshape(DB, TS, D)
    st = lambda n: jnp.stack(outs[n])
    return (yp, ys, st("kp"), st("vp"), st("ikp"), st("sp"), st("ks"), st("vs"), st("iks"), st("ss"))
```

```python
import functools
import math

import jax
import jax.numpy as jnp
import numpy as np
from jax import lax
from jax.experimental import pallas as pl
from jax.experimental.pallas import tpu as pltpu

F32 = jnp.float32
BF16 = jnp.bfloat16
I32 = jnp.int32

ATTN_HEADS = 8
ATTN_KV_HEADS = 2
HEAD_DIM = 64
ATTN_WIDTH = ATTN_HEADS * HEAD_DIM
KV_WIDTH = ATTN_KV_HEADS * HEAD_DIM
IDX_HEADS = 8
IDX_DIM = 64
IDX_TOPK_MAX = 256
GLA_HEADS = 4
GLA_DK = 64
GLA_DV = 128
GLA_K_WIDTH = GLA_HEADS * GLA_DK
GLA_V_WIDTH = GLA_HEADS * GLA_DV
GLA_GATE_RANK = 16
GLA_GATE_TAU = 16.0
GLA_CHUNK = 64
GLA_SUB = 32
REL_BUCKETS = 32
REL_MAX_DIST = 128
RMS_EPS = 1e-6
LN_EPS = 1e-5

LANES = 128
SUBLANES = 8
VMEM_LIMIT = 56 * 1024 * 1024

TOKEN_TILE = 512
FFN_CHUNK = 256
ATTN_TILE = 256
FAR_SPAN = 4
KEY_SLAB = 32
COARSE_SLAB = 64
SOFTMAX_SLAB = 32
GLA_STEP = 512

INT_MIN = np.int32(-2 ** 31)
INT_MAX = np.int32(2 ** 31 - 1)
NEG = -1e30
LOG2E = math.log2(math.e)
Q_SCALE = HEAD_DIM ** -0.5 * LOG2E

C_Q = 0
C_K = 512
C_V = 640
C_IQ = 768
C_IK = 1280
C_IW = 1408
C_GQ = 1536
C_GK = 1792
C_GV = 2048
C_GR = 2560
C_GLR = 3072
C_END = 3200


def _nt(a, b):
    return lax.dot_general(a, b, (((1,), (1,)), ((), ())), preferred_element_type=F32)


def _tn(a, b, precision=None):
    return lax.dot_general(a, b, (((0,), (0,)), ((), ())), preferred_element_type=F32,
                           precision=precision)


def _rep(x, n):
    return x if n == 1 else jnp.concatenate([x] * n, axis=1)


def _const_spec(shape):
    nd = len(shape)
    return pl.BlockSpec(shape, lambda *_: (0,) * nd, pipeline_mode=pl.Buffered(1))


def _params(sem):
    return pltpu.CompilerParams(dimension_semantics=sem, vmem_limit_bytes=VMEM_LIMIT)


def _inproj_kernel(x_ref, g_ref, w_ref, lng_ref, lnb_ref, gup_ref, gb_ref,
                   q_ref, k_ref, v_ref, kb_ref, vb_ref, iq_ref, ik_ref, ik2_ref, iw_ref,
                   gq_ref, gk_ref, gv_ref, gr_ref, la_ref):
    x = x_ref[...]
    ms = jnp.mean(x * x, axis=-1, keepdims=True)
    h = (x * lax.rsqrt(ms + RMS_EPS) * g_ref[...]).astype(BF16)

    def mm(a, b):
        return jnp.dot(h, w_ref[:, a:b], preferred_element_type=F32)

    q_ref[...] = (mm(C_Q, C_K) * Q_SCALE).astype(BF16)
    k = mm(C_K, C_V)
    k_ref[...] = k
    kb_ref[...] = k.astype(BF16)
    v = mm(C_V, C_IQ)
    v_ref[...] = v
    vb_ref[...] = v.astype(BF16)
    iq_ref[...] = mm(C_IQ, C_IK).astype(BF16)

    z = mm(C_IK, C_IW)
    mu = jnp.mean(z, axis=-1, keepdims=True)
    zc = z - mu
    var = jnp.mean(zc * zc, axis=-1, keepdims=True)
    ikn = zc * lax.rsqrt(var + LN_EPS) * lng_ref[...] + lnb_ref[...]
    ik_ref[...] = ikn[:, :IDX_DIM]
    ik2_ref[...] = ikn.astype(BF16)

    iw_ref[...] = mm(C_IW, C_GQ) * (IDX_HEADS ** -0.5 * IDX_DIM ** -0.5)
    gq_ref[...] = mm(C_GQ, C_GK) * (GLA_DK ** -0.5)
    gk_ref[...] = mm(C_GK, C_GV)
    gv_ref[...] = mm(C_GV, C_GR)
    gr_ref[...] = mm(C_GR, C_GLR)
    glr = mm(C_GLR, C_END).astype(BF16)
    gate = jnp.dot(glr, gup_ref[...], preferred_element_type=F32) + gb_ref[...]
    log_sig = jnp.minimum(gate, 0.0) - jnp.log1p(jnp.exp(-jnp.abs(gate)))
    la_ref[...] = log_sig / GLA_GATE_TAU


def _inproj(x, g, w, lng, lnb, gup, gb):
    n, d = x.shape
    tm = TOKEN_TILE
    row = lambda c: pl.BlockSpec((tm, c), lambda i: (i, 0))
    outs = [(ATTN_WIDTH, BF16), (KV_WIDTH, F32), (KV_WIDTH, F32), (KV_WIDTH, BF16), (KV_WIDTH, BF16),
            (IDX_HEADS * IDX_DIM, BF16), (IDX_DIM, F32), (2 * IDX_DIM, BF16), (LANES, F32),
            (GLA_K_WIDTH, F32), (GLA_K_WIDTH, F32), (GLA_V_WIDTH, F32), (GLA_V_WIDTH, F32),
            (GLA_K_WIDTH, F32)]
    return pl.pallas_call(
        _inproj_kernel,
        grid=(n // tm,),
        in_specs=[row(d), _const_spec((1, d)), _const_spec(w.shape), _const_spec(lng.shape),
                  _const_spec(lnb.shape), _const_spec(gup.shape), _const_spec(gb.shape)],
        out_specs=[row(c) for c, _ in outs],
        out_shape=[jax.ShapeDtypeStruct((n, c), dt) for c, dt in outs],
        compiler_params=_params(("parallel",)),
        name="inproj",
    )(x, g, w, lng, lnb, gup, gb)


def _ffn_kernel(x_ref, ao_ref, go_ref, wo_ref, g_ref, wg_ref, wu_ref, wd_ref, o_ref):
    half = ao_ref.shape[1]
    x1 = (x_ref[...]
          + jnp.dot(ao_ref[...], wo_ref[:half, :], preferred_element_type=F32)
          + jnp.dot(go_ref[...], wo_ref[half:, :], preferred_element_type=F32))
    ms = jnp.mean(x1 * x1, axis=-1, keepdims=True)
    h = (x1 * lax.rsqrt(ms + RMS_EPS) * g_ref[...]).astype(BF16)
    acc = None
    d_ff = wg_ref.shape[1]
    for c in range(0, d_ff, FFN_CHUNK):
        gt = jnp.dot(h, wg_ref[:, c:c + FFN_CHUNK], preferred_element_type=F32)
        up = jnp.dot(h, wu_ref[:, c:c + FFN_CHUNK], preferred_element_type=F32)
        u = (gt * jax.nn.sigmoid(gt) * up).astype(BF16)
        part = jnp.dot(u, wd_ref[c:c + FFN_CHUNK, :], preferred_element_type=F32)
        acc = part if acc is None else acc + part
    o_ref[...] = x1 + acc


def _ffn(x, ao, go, wo, g, wg, wu, wd):
    n, d = x.shape
    tm = TOKEN_TILE
    assert wg.shape[1] % FFN_CHUNK == 0
    row = lambda c: pl.BlockSpec((tm, c), lambda i: (i, 0))
    return pl.pallas_call(
        _ffn_kernel,
        grid=(n // tm,),
        in_specs=[row(d), row(ao.shape[1]), row(go.shape[1]), _const_spec(wo.shape),
                  _const_spec((1, d)), _const_spec(wg.shape), _const_spec(wu.shape),
                  _const_spec(wd.shape)],
        out_specs=row(d),
        out_shape=jax.ShapeDtypeStruct((n, d), F32),
        compiler_params=_params(("parallel",)),
        name="outproj_ffn",
    )(x, ao, go, wo, g, wg, wu, wd)


def _rmsnorm_kernel(x_ref, g_ref, o_ref):
    x = x_ref[...]
    ms = jnp.mean(x * x, axis=-1, keepdims=True)
    o_ref[...] = x * lax.rsqrt(ms + RMS_EPS) * g_ref[...]


def _rmsnorm(x, g):
    n, d = x.shape
    tm = TOKEN_TILE
    return pl.pallas_call(
        _rmsnorm_kernel,
        grid=(n // tm,),
        in_specs=[pl.BlockSpec((tm, d), lambda i: (i, 0)), _const_spec((1, d))],
        out_specs=pl.BlockSpec((tm, d), lambda i: (i, 0)),
        out_shape=jax.ShapeDtypeStruct((n, d), F32),
        compiler_params=_params(("parallel",)),
        name="final_norm",
    )(x, g)


def _rel_bucket(dist):
    n = jnp.maximum(dist, 0)
    max_exact = REL_BUCKETS // 2
    large = max_exact + (jnp.log(jnp.maximum(n, 1).astype(F32) / max_exact)
                         / math.log(REL_MAX_DIST / max_exact)
                         * (REL_BUCKETS - max_exact)).astype(I32)
    large = jnp.minimum(large, REL_BUCKETS - 1)
    return jnp.where(n < max_exact, n, large)


def _bias_kernel(rb_ref, o_ref, *, offsets, key_major):
    _, nh, r, c = o_ref.shape
    row = lax.broadcasted_iota(I32, (r, c), 0)
    col = lax.broadcasted_iota(I32, (r, c), 1)
    for t, off in enumerate(offsets):
        bucket = _rel_bucket((col - row if key_major else row - col) + off)
        for h in range(nh):
            val = jnp.zeros((r, c), F32)
            for b in range(REL_BUCKETS):
                val = jnp.where(bucket == b, rb_ref[b, h], val)
            o_ref[t, h] = val * LOG2E


def _bias_tables(rel_bias, rows, cols, offsets, key_major=False):
    nh = rel_bias.shape[1]
    return pl.pallas_call(
        functools.partial(_bias_kernel, offsets=tuple(offsets), key_major=key_major),
        in_specs=[pl.BlockSpec(memory_space=pltpu.SMEM)],
        out_shape=jax.ShapeDtypeStruct((len(offsets), nh, rows, cols), F32),
        name="bias_tables",
    )(rel_bias)


def _score_key(s):
    bits = lax.bitcast_convert_type(s, I32)
    return bits ^ ((bits >> 31) & INT_MAX)


TINY_F32 = float(np.finfo(np.float32).tiny)


def _key_score(key):
    return lax.bitcast_convert_type(key ^ ((key >> 31) & INT_MAX), F32)


def _float_key(x):
    bits = int(np.float32(x).view(np.int32))
    return np.int32(bits ^ ((bits >> 31) & 0x7FFFFFFF))


def _topk_threshold(count_ge, rows, topk):
    def step(it, base):
        shift = 30 - 2 * it
        u = base
        for digit in (1, 2, 3):
            cand = base | lax.shift_left(jnp.int32(digit), shift)
            u = jnp.where(count_ge(cand ^ INT_MIN) >= topk, cand, u)
        return u

    u = lax.fori_loop(0, 16, step, jnp.zeros((rows, LANES), I32))
    return u ^ INT_MIN


def _tie_limit(count_eq_below, need, rows, idx_bits):
    def bit_body(it, p):
        bit = lax.shift_left(jnp.int32(1), idx_bits - 1 - it)
        p2 = p | bit
        cnt = count_eq_below(p2)
        return jnp.where(cnt < need, p2, p)

    return lax.fori_loop(0, idx_bits, bit_body, jnp.zeros((rows, LANES), I32))


def _dsa_prompt_kernel(iq_ref, iwt_ref, q_ref, ik2_ref, k_ref, vt_ref, btab_ref, cfar_ref,
                         o_ref,
                         skey_ref, sbf_ref, iqm_ref, qm_ref, m_ref, l_ref, acc_ref, thr_ref, p_ref,
                         madd_ref, lg0_ref, lg1_ref, pb0_ref, pb1_ref,
                         *, tq, topk, idx_bits):
    i = pl.program_id(1)
    sl = SUBLANES
    slab = KEY_SLAB
    cslab = COARSE_SLAB
    lane = lax.broadcasted_iota(I32, (1, LANES), 1)
    lo_half = lane < HEAD_DIM
    key_id = lax.broadcasted_iota(I32, (tq, tq), 0)
    qry_id = lax.broadcasted_iota(I32, (tq, tq), 1)
    slab_id = lax.broadcasted_iota(I32, (slab, tq), 0)

    for h in range(IDX_HEADS):
        grp = iq_ref[:, LANES * (h // 2):LANES * (h // 2 + 1)]
        iqm_ref[h] = jnp.where(lo_half if h % 2 == 0 else ~lo_half, grp, jnp.zeros_like(grp))
    for h in range(ATTN_HEADS):
        p, g = h % 4, h // 4
        grp = q_ref[:, LANES * p:LANES * (p + 1)]
        qm_ref[h] = jnp.where(lo_half if g == 0 else ~lo_half, grp, jnp.zeros_like(grp))
        m_ref[h] = jnp.full((sl, tq), NEG, F32)
        l_ref[h] = jnp.zeros((sl, tq), F32)
        acc_ref[h] = jnp.zeros((KV_WIDTH, tq), F32)

    def key_rows(j, n=1):
        return pl.ds(pl.multiple_of(j * tq, tq), n * tq)

    def score_tile(j, diag):
        ikt = ik2_ref[key_rows(j), :]
        s = jnp.zeros((tq, tq), F32)
        for h in range(IDX_HEADS):
            d = _nt(ikt, iqm_ref[h])
            s = s + iwt_ref[h:h + 1, :] * jnp.maximum(d, 0.0)
        bits = lax.bitcast_convert_type(s, I32)
        key = bits ^ ((bits >> 31) & INT_MAX)
        coarse = lax.bitcast_convert_type(bits & jnp.int32(-65536), F32)
        if diag:
            s = jnp.where(key_id <= qry_id, s, -jnp.inf)
            coarse = jnp.where(key_id <= qry_id, coarse, -jnp.inf)
        skey_ref[j] = s
        sbf_ref[j] = coarse.astype(BF16)

    def score_body(j, c):
        score_tile(j, False)
        return c

    lax.fori_loop(0, i, score_body, 0)
    score_tile(i, True)

    def tall(x, rows=slab):
        return jnp.concatenate([x] * (rows // sl), axis=0)

    def count_keys(pred):
        def body(j, acc):
            for r in range(tq // slab):
                acc = acc + jnp.where(pred(skey_ref[j, r * slab:(r + 1) * slab, :],
                                           j * tq + r * slab), 1.0, 0.0)
            return acc
        acc = lax.fori_loop(0, i + 1, body, jnp.zeros((slab, tq), F32))
        return jnp.broadcast_to(jnp.sum(acc, axis=0, keepdims=True), (sl, tq))

    def count_coarse(cand):
        one = jnp.ones((cslab, tq), BF16)
        zero = jnp.zeros((cslab, tq), BF16)

        def body(j, acc):
            for r in range(tq // cslab):
                acc = acc + jnp.where(sbf_ref[j, r * cslab:(r + 1) * cslab, :] >= cand, one, zero)
            return acc
        acc = lax.fori_loop(0, i + 1, body, zero)
        return jnp.broadcast_to(jnp.sum(acc.astype(F32), axis=0, keepdims=True), (sl, tq))

    def coarse_value(u):
        code = lax.shift_right_arithmetic(lax.shift_left(u ^ 0x8000, 16), 16)
        hi = code ^ (lax.shift_right_arithmetic(code, 31) & 0x7FFF)
        return lax.bitcast_convert_type(lax.shift_left(hi, 16), F32)

    def coarse_body(it, u):
        u2 = u | lax.shift_left(jnp.int32(1), 15 - it)
        cand = tall(coarse_value(u2), cslab).astype(BF16)
        return jnp.where(count_coarse(cand) >= topk, u2, u)

    u16 = lax.fori_loop(0, 16, coarse_body, jnp.zeros((sl, tq), I32))
    t16 = coarse_value(u16)
    real_thr = t16 > -jnp.inf
    tiny = jnp.abs(t16) < TINY_F32
    k32 = lax.shift_left(u16 ^ 0x8000, 16)
    lo0 = jnp.where(tiny, _float_key(-TINY_F32), k32 - 0x10000)
    hi0 = jnp.where(tiny, _float_key(TINY_F32), k32 + 0x10000)
    lo0 = jnp.where(real_thr, lo0, INT_MIN)
    hi0 = jnp.where(real_thr, hi0, INT_MIN + 1)
    c_lo0 = count_keys(lambda kt, _: kt >= tall(_key_score(lo0)))

    def active(lo, hi, c_lo):
        return jnp.logical_and(hi - lo > 1, c_lo != topk)

    def bisect_cond(carry):
        lo, hi, c_lo = carry
        return jnp.max(jnp.where(active(lo, hi, c_lo), 1.0, 0.0)) > 0.0

    def bisect_step(carry):
        lo, hi, c_lo = carry
        mid = lo + lax.shift_right_arithmetic(hi - lo, 1)
        mid_t = tall(_key_score(mid))
        c_mid = count_keys(lambda kt, _: kt >= mid_t)
        act = active(lo, hi, c_lo)
        up = jnp.logical_and(act, c_mid >= topk)
        down = jnp.logical_and(act, c_mid < topk)
        return (jnp.where(up, mid, lo), jnp.where(down, mid, hi), jnp.where(up, c_mid, c_lo))

    thr, _, cnt_ge = lax.while_loop(bisect_cond, lambda c: bisect_step(bisect_step(c)),
                                    (lo0, hi0, c_lo0))
    thr = jnp.where(real_thr, _key_score(thr), -jnp.inf)
    thr_t = tall(thr)
    tie = jnp.logical_and(cnt_ge > topk, real_thr)
    thr_ref[...] = thr
    p_ref[...] = jnp.where(real_thr, INT_MAX, jnp.int32(-1))

    @pl.when(jnp.max(jnp.where(tie, 1.0, 0.0)) > 0.0)
    def _():
        need = topk - count_keys(lambda kt, _: kt > thr_t)

        def tie_body(it, p):
            p2 = p | lax.shift_left(jnp.int32(1), idx_bits - 1 - it)
            p2_t = tall(p2)
            cnt = count_keys(lambda kt, base: jnp.logical_and(kt == thr_t, base + slab_id < p2_t))
            return jnp.where(cnt < need, p2, p)

        plim = lax.fori_loop(0, idx_bits, tie_body, jnp.zeros((sl, tq), I32))
        p_ref[...] = jnp.where(tie, plim, p_ref[...])

    def attn_span(j0, nt, near):
        nkeys = nt * tq
        kb = k_ref[key_rows(j0, nt), :]
        vt = vt_ref[:, key_rows(j0, nt)]
        thr_t = tall(thr_ref[...])
        plim_t = tall(p_ref[...])

        def mask_body(r, c):
            r0 = pl.multiple_of(r * slab, slab)
            for t in range(nt):
                kt = skey_ref[j0 + t, pl.ds(r0, slab), :]
                idx = (j0 + t) * tq + r0 + slab_id
                sel = jnp.logical_or(kt > thr_t, jnp.logical_and(kt == thr_t, idx <= plim_t))
                madd_ref[pl.ds(t * tq + r0, slab), :] = jnp.where(sel, 0.0, NEG)
            return c

        lax.fori_loop(0, tq // slab, mask_body, 0)

        bufs = ((lg0_ref, pb0_ref), (lg1_ref, pb1_ref))
        chunk = LANES

        def logits_chunk(hh, c):
            rows = slice(c * chunk, (c + 1) * chunk)
            bufs[hh % 2][0][rows, :] = _nt(kb[rows], qm_ref[hh])

        for c in range(nkeys // chunk):
            logits_chunk(0, c)
        ss = SOFTMAX_SLAB
        per_chunk = chunk // ss
        for h in range(ATTN_HEADS):
            lg_ref, pb_ref = bufs[h % 2]
            mx = jnp.full((ss, tq), NEG, F32)
            for r in range(nkeys // ss):
                if h + 1 < ATTN_HEADS and r % per_chunk == 0:
                    logits_chunk(h + 1, r // per_chunk)
                rows = slice(r * ss, (r + 1) * ss)
                lg = lg_ref[rows, :] + madd_ref[rows, :]
                if near is not None:
                    lg = lg + (btab_ref[near, h, rows, :] - cfar_ref[h])
                lg_ref[rows, :] = lg
                mx = jnp.maximum(mx, lg)
            m_old = m_ref[h]
            m_new = jnp.maximum(m_old, jnp.max(mx, axis=0, keepdims=True))
            alpha = jnp.exp2(m_old - m_new)
            m_ref[h] = m_new
            m_t = tall(m_new, ss)
            ps = jnp.zeros((ss, tq), F32)
            for r in range(nkeys // ss):
                rows = slice(r * ss, (r + 1) * ss)
                pexp = jnp.exp2(lg_ref[rows, :] - m_t)
                ps = ps + pexp
                pb_ref[rows, :] = pexp.astype(BF16)
            psum = ps[0:sl]
            for r in range(1, ss // sl):
                psum = psum + ps[r * sl:(r + 1) * sl]
            l_ref[h] = alpha * l_ref[h] + psum
            acc_ref[h] = alpha[0:1, :] * acc_ref[h] + jnp.dot(vt, pb_ref[:nkeys, :],
                                                              preferred_element_type=F32)

    n_far = jnp.maximum(i - 1, 0)
    n_span = n_far // FAR_SPAN

    def far_span_body(s, c):
        attn_span(s * FAR_SPAN, FAR_SPAN, None)
        return c

    lax.fori_loop(0, n_span, far_span_body, 0)

    done = n_span * FAR_SPAN
    if FAR_SPAN >= 4:
        half = FAR_SPAN // 2
        n_half = (n_far - done) // half

        def far_half_body(s, c):
            attn_span(done + s * half, half, None)
            return c

        lax.fori_loop(0, n_half, far_half_body, 0)
        done = done + n_half * half

    def far_body(j, c):
        attn_span(j, 1, None)
        return c

    lax.fori_loop(done, n_far, far_body, 0)

    @pl.when(i >= 1)
    def _():
        attn_span(i - 1, 1, 0)

    attn_span(i, 1, 1)

    dim_id = lax.broadcasted_iota(I32, (KV_WIDTH, tq), 0)
    for p in range(4):
        o_lo = acc_ref[p] / jnp.sum(l_ref[p], axis=0, keepdims=True)
        o_hi = acc_ref[p + 4] / jnp.sum(l_ref[p + 4], axis=0, keepdims=True)
        o_t = jnp.where(dim_id < HEAD_DIM, o_lo, o_hi)
        o_ref[:, LANES * p:LANES * (p + 1)] = o_t.T.astype(o_ref.dtype)


def _dsa_prompt(iq, iwt, q, ik2, kb, vt, btab, cfar, batch, seq, topk):
    tq = min(ATTN_TILE, seq)
    assert seq % tq == 0 and tq % LANES == 0
    nq = seq // tq
    idx_bits = max(1, int(seq).bit_length())
    span = FAR_SPAN * tq
    qrow = lambda c: pl.BlockSpec((tq, c), lambda b, i: (b * nq + i, 0))
    seqblk = lambda c: pl.BlockSpec((seq, c), lambda b, i: (b, 0))
    kern = functools.partial(_dsa_prompt_kernel, tq=tq, topk=float(topk), idx_bits=idx_bits)
    return pl.pallas_call(
        kern,
        grid=(batch, nq),
        in_specs=[qrow(IDX_HEADS * IDX_DIM),
                  pl.BlockSpec((IDX_HEADS, tq), lambda b, i: (0, b * nq + i)),
                  qrow(ATTN_WIDTH), seqblk(2 * IDX_DIM), seqblk(KV_WIDTH),
                  pl.BlockSpec((KV_WIDTH, seq), lambda b, i: (0, b)),
                  _const_spec(btab.shape), pl.BlockSpec(memory_space=pltpu.SMEM)],
        out_specs=qrow(ATTN_WIDTH),
        out_shape=jax.ShapeDtypeStruct((batch * seq, ATTN_WIDTH), BF16),
        scratch_shapes=[pltpu.VMEM((nq, tq, tq), F32),
                        pltpu.VMEM((nq, tq, tq), BF16),
                        pltpu.VMEM((IDX_HEADS, tq, LANES), BF16),
                        pltpu.VMEM((ATTN_HEADS, tq, LANES), BF16),
                        pltpu.VMEM((ATTN_HEADS, SUBLANES, tq), F32),
                        pltpu.VMEM((ATTN_HEADS, SUBLANES, tq), F32),
                        pltpu.VMEM((ATTN_HEADS, KV_WIDTH, tq), F32),
                        pltpu.VMEM((SUBLANES, tq), F32),
                        pltpu.VMEM((SUBLANES, tq), I32),
                        pltpu.VMEM((span, tq), F32),
                        pltpu.VMEM((span, tq), F32),
                        pltpu.VMEM((span, tq), F32),
                        pltpu.VMEM((span, tq), BF16),
                        pltpu.VMEM((span, tq), BF16)],
        compiler_params=_params(("parallel", "arbitrary")),
        name="dsa_prompt",
    )(iq, iwt, q, ik2, kb, vt, btab, cfar)


def _dsa_sample_kernel(pt_ref, iq_ref, iwr_ref, q_ref, ikn_ref, kn_ref, vn_ref, bias_ref, *rest,
                       n_pages, page, ts, topk, idx_bits):
    ikp = rest[:n_pages]
    kp = rest[n_pages:2 * n_pages]
    vp = rest[2 * n_pages:3 * n_pages]
    o_ref = rest[3 * n_pages]
    skey_ref, lg_ref, p_ref = rest[3 * n_pages + 1:]
    del pt_ref
    nt = n_pages + 1
    lane = lax.broadcasted_iota(I32, (1, LANES), 1)
    lo_half = lane < HEAD_DIM
    iq = iq_ref[...]
    iwr = iwr_ref[...]
    qh = q_ref[...]

    def score(keys_t):
        d = jnp.dot(iq, keys_t, preferred_element_type=F32)
        w = iwr * jnp.maximum(d, 0.0)
        s = w[0:ts]
        for h in range(1, IDX_HEADS):
            s = s + w[h * ts:(h + 1) * ts]
        return s

    for pg in range(n_pages):
        skey_ref[:, pg * page:(pg + 1) * page] = score(ikp[pg][...].astype(BF16))
    new_key = score(ikn_ref[...].astype(BF16))
    qrow = lax.broadcasted_iota(I32, (ts, page), 0)
    qcol = lax.broadcasted_iota(I32, (ts, page), 1)
    skey_ref[:, n_pages * page:] = jnp.where(qcol <= qrow, new_key, -jnp.inf)

    def count_tiles(pred):
        acc = jnp.zeros((ts, LANES), F32)
        for c in range(nt):
            acc = acc + jnp.where(pred(skey_ref[:, c * LANES:(c + 1) * LANES], c * LANES), 1.0, 0.0)
        return jnp.sum(acc, axis=1, keepdims=True)

    thr = _key_score(_topk_threshold(
        lambda cand: count_tiles(lambda kt, _: kt >= _key_score(cand)), ts, topk))
    cnt_ge = count_tiles(lambda kt, _: kt >= thr)
    real_thr = thr[:, :1] > -jnp.inf
    tie = jnp.logical_and(cnt_ge > topk, real_thr)
    p_ref[...] = jnp.broadcast_to(jnp.where(real_thr, INT_MAX, jnp.int32(-1)), (ts, LANES))

    @pl.when(jnp.max(jnp.where(tie, 1.0, 0.0)) > 0.0)
    def _():
        need = topk - count_tiles(lambda kt, _: kt > thr)

        def count_eq_below(p):
            return count_tiles(lambda kt, base: jnp.logical_and(kt == thr, base + lane < p))

        p_ref[...] = jnp.where(tie, _tie_limit(count_eq_below, need, ts, idx_bits), p_ref[...])

    plim = p_ref[...]

    ktot = nt * page
    for pg in range(n_pages):
        lg_ref[:, pg * page:(pg + 1) * page] = jnp.dot(qh, kp[pg][...].astype(BF16),
                                                       preferred_element_type=F32)
    lg_ref[:, n_pages * page:] = jnp.dot(qh, kn_ref[...], preferred_element_type=F32)
    kt = skey_ref[...]
    thr_full = _rep(thr, ktot // LANES)
    idx = lax.broadcasted_iota(I32, (ts, ktot), 1)
    sel = jnp.logical_or(kt > thr_full,
                         jnp.logical_and(kt == thr_full, idx <= _rep(plim, ktot // LANES)))
    sel = jnp.concatenate([sel.astype(F32)] * ATTN_HEADS, axis=0) > 0.5
    lg = lg_ref[...] + bias_ref[...]
    lg = jnp.where(sel, lg, NEG)
    m = jnp.max(lg, axis=1, keepdims=True)
    pexp = jnp.exp2(lg - m)
    denom = jnp.sum(pexp, axis=1, keepdims=True)
    pb = pexp.astype(BF16)
    acc = _nt(pb[:, n_pages * page:], vn_ref[...])
    for pg in range(n_pages):
        acc = acc + _nt(pb[:, pg * page:(pg + 1) * page], vp[pg][...].astype(BF16))
    o = acc / denom
    half = (ATTN_HEADS // 2) * ts
    o_ref[...] = jnp.where(lo_half, o[:half], o[half:]).astype(o_ref.dtype)


def _dsa_sample(page_table, iq, iwr, qh, ikn, kn, vn, bias, cidx, ck, cv, layer, topk):
    nseq, n_pages = page_table.shape
    page = cidx.shape[3]
    ts = iq.shape[1] // IDX_HEADS
    assert page == LANES
    idx_bits = int((n_pages + 1) * page).bit_length()
    seq3 = lambda a: pl.BlockSpec((None,) + a.shape[1:], lambda b, pt: (b, 0, 0))

    def page_spec(a, pg):
        return pl.BlockSpec((None, None) + a.shape[2:], lambda b, pt: (layer, pt[b, pg], 0, 0))

    kern = functools.partial(_dsa_sample_kernel, n_pages=n_pages, page=page, ts=ts,
                             topk=float(topk), idx_bits=idx_bits)
    ktot = (n_pages + 1) * page
    in_specs = ([seq3(iq), seq3(iwr), seq3(qh), seq3(ikn), seq3(kn), seq3(vn),
                 pl.BlockSpec(bias.shape, lambda b, pt: (0, 0))]
                + [page_spec(cidx, pg) for pg in range(n_pages)]
                + [page_spec(ck, pg) for pg in range(n_pages)]
                + [page_spec(cv, pg) for pg in range(n_pages)])
    rows_out = (ATTN_HEADS // 2) * ts
    return pl.pallas_call(
        kern,
        grid_spec=pltpu.PrefetchScalarGridSpec(
            num_scalar_prefetch=1,
            grid=(nseq,),
            in_specs=in_specs,
            out_specs=pl.BlockSpec((None, rows_out, LANES), lambda b, pt: (b, 0, 0)),
            scratch_shapes=[pltpu.VMEM((ts, ktot), F32),
                            pltpu.VMEM((ATTN_HEADS * ts, ktot), F32),
                            pltpu.VMEM((ts, LANES), I32)]),
        out_shape=jax.ShapeDtypeStruct((nseq, rows_out, LANES), BF16),
        compiler_params=_params(("arbitrary",)),
        name="dsa_sample",
    )(page_table, iq, iwr, qh, ikn, kn, vn, bias,
      *([cidx] * n_pages), *([ck] * n_pages), *([cv] * n_pages))


def _gla_chunk(q_ref, k_ref, v_ref, la_ref, r_ref, g_ref, o_ref, s_ref, b_scr, o_scr,
               row0, chunk, sub):
    kw = GLA_K_WIDTH
    rows = pl.ds(row0, chunk)
    q = q_ref[rows, :]
    k = k_ref[rows, :]
    v = v_ref[rows, :]
    la = la_ref[rows, :]
    hi = lax.Precision.HIGHEST
    tri = (lax.broadcasted_iota(I32, (chunk, chunk), 0)
           >= lax.broadcasted_iota(I32, (chunk, chunk), 1)).astype(F32)
    b = jnp.dot(tri, la, preferred_element_type=F32, precision=hi)
    b_scr[...] = b
    lane_head = lax.broadcasted_iota(I32, (1, kw), 1) // GLA_DK
    row_head = lax.broadcasted_iota(I32, (kw, GLA_DV), 0) // GLA_DK
    s_old = s_ref[...]
    s_bf = s_old.astype(BF16)

    qe = q * jnp.exp(b)
    b_last_col = _tn(la, jnp.ones((chunk, GLA_DV), F32), precision=hi)
    b_last = b[chunk - 1:chunk, :]
    ke = (k * jnp.exp(b_last - b)).astype(BF16)
    s_new = jnp.exp(b_last_col) * s_old
    for h in range(GLA_HEADS):
        qh = jnp.where(lane_head == h, qe, 0.0).astype(BF16)
        o_scr[:, h * GLA_DV:(h + 1) * GLA_DV] = jnp.dot(qh, s_bf, preferred_element_type=F32)
        kv = _tn(ke, v[:, h * GLA_DV:(h + 1) * GLA_DV].astype(BF16))
        s_new = s_new + jnp.where(row_head == h, kv, 0.0)
    s_ref[...] = s_new

    nsub = chunk // sub
    seg = (lax.broadcasted_iota(I32, (kw, GLA_V_WIDTH), 0) // GLA_DK
           == lax.broadcasted_iota(I32, (kw, GLA_V_WIDTH), 1) // GLA_DV).astype(BF16)
    t_id = lax.broadcasted_iota(I32, (sub, kw), 0)
    v_bf = v.astype(BF16)
    col = lax.broadcasted_iota(I32, (sub, chunk), 1)

    def sub_body(i, c):
        r0 = i * sub
        srows = pl.ds(row0 + r0, sub)
        qb = q_ref[srows, :]
        kb = k_ref[srows, :]
        vb = v_ref[srows, :]
        bb = b_scr[pl.ds(r0, sub), :]
        base = bb[0:1, :] - la_ref[pl.ds(row0 + r0, 1), :]
        o_blk = o_scr[pl.ds(r0, sub), :]
        o_h = [o_blk[:, h * GLA_DV:(h + 1) * GLA_DV] for h in range(GLA_HEADS)]
        if r0 > 0:
            qi = qb * jnp.exp(bb - base)
            kpre = (k[:r0] * jnp.exp(base - b[:r0])).astype(BF16)
            for h in range(GLA_HEADS):
                qih = jnp.where(lane_head == h, qi, 0.0).astype(BF16)
                a = _nt(qih, kpre).astype(BF16)
                o_h[h] = o_h[h] + jnp.dot(a, v_bf[:r0, h * GLA_DV:(h + 1) * GLA_DV],
                                          preferred_element_type=F32)
        o_blk = jnp.concatenate(o_h, axis=1)
        prods = []
        for s in range(sub):
            e = jnp.exp(jnp.minimum(bb - bb[s:s + 1, :], 0.0))
            prods.append(jnp.where(t_id >= s, qb * kb[s:s + 1, :] * e, 0.0))
        pall = jnp.concatenate(prods, axis=0).astype(BF16)
        abc = jnp.dot(pall, seg, preferred_element_type=F32)
        for s in range(sub):
            o_blk = o_blk + abc[s * sub:(s + 1) * sub, :] * vb[s:s + 1, :]
        r = r_ref[srows, :]
        outs = []
        for h in range(GLA_HEADS):
            oh = o_blk[:, h * GLA_DV:(h + 1) * GLA_DV]
            ms = jnp.mean(oh * oh, axis=-1, keepdims=True)
            outs.append(oh * lax.rsqrt(ms + RMS_EPS) * g_ref[...])
        y = jnp.concatenate(outs, axis=1)
        o_ref[srows, :] = (y * (r * jax.nn.sigmoid(r))).astype(o_ref.dtype)
        return c

    for i in range(nsub):
        sub_body(i, 0)


def _gla_prompt_kernel(q_ref, k_ref, v_ref, la_ref, r_ref, g_ref, o_ref, sout_ref,
                       s_ref, b_scr, o_scr, *, chunk, sub, n_chunks):
    @pl.when(pl.program_id(1) == 0)
    def _():
        s_ref[...] = jnp.zeros_like(s_ref)

    def body(c, carry):
        _gla_chunk(q_ref, k_ref, v_ref, la_ref, r_ref, g_ref, o_ref, s_ref, b_scr, o_scr,
                   pl.multiple_of(c * chunk, chunk), chunk, sub)
        return carry

    lax.fori_loop(0, n_chunks, body, 0)
    sout_ref[...] = s_ref[...]


def _gla_prompt(gq, gk, gv, la, gr, g, batch, seq):
    step = min(GLA_STEP, seq)
    chunk = min(GLA_CHUNK, seq)
    sub = min(GLA_SUB, chunk)
    assert seq % step == 0 and step % chunk == 0 and chunk % sub == 0
    ns = seq // step
    row = lambda c: pl.BlockSpec((step, c), lambda b, i: (b * ns + i, 0))
    kern = functools.partial(_gla_prompt_kernel, chunk=chunk, sub=sub, n_chunks=step // chunk)
    return pl.pallas_call(
        kern,
        grid=(batch, ns),
        in_specs=[row(GLA_K_WIDTH), row(GLA_K_WIDTH), row(GLA_V_WIDTH), row(GLA_K_WIDTH),
                  row(GLA_V_WIDTH), _const_spec((1, GLA_DV))],
        out_specs=[row(GLA_V_WIDTH),
                   pl.BlockSpec((None, GLA_K_WIDTH, GLA_DV), lambda b, i: (b, 0, 0))],
        out_shape=[jax.ShapeDtypeStruct((batch * seq, GLA_V_WIDTH), BF16),
                   jax.ShapeDtypeStruct((batch, GLA_K_WIDTH, GLA_DV), F32)],
        scratch_shapes=[pltpu.VMEM((GLA_K_WIDTH, GLA_DV), F32),
                        pltpu.VMEM((chunk, GLA_K_WIDTH), F32),
                        pltpu.VMEM((chunk, GLA_V_WIDTH), F32)],
        compiler_params=_params(("parallel", "arbitrary")),
        name="gla_prompt",
    )(gq, gk, gv, la, gr, g)


def _gla_sample_kernel(q_ref, k_ref, v_ref, la_ref, r_ref, g_ref, s0_ref, o_ref, sout_ref,
                       s_ref, b_scr, o_scr, *, chunk):
    s_ref[...] = s0_ref[...]
    _gla_chunk(q_ref, k_ref, v_ref, la_ref, r_ref, g_ref, o_ref, s_ref, b_scr, o_scr,
               0, chunk, chunk)
    sout_ref[...] = s_ref[...]


def _gla_sample(gq, gk, gv, la, gr, g, s0, row_off, nseq, ts):
    assert row_off % ts == 0 and ts <= GLA_CHUNK and ts % SUBLANES == 0
    blk0 = row_off // ts
    row = lambda c: pl.BlockSpec((ts, c), lambda b: (blk0 + b, 0))
    st = pl.BlockSpec((None, GLA_K_WIDTH, GLA_DV), lambda b: (b, 0, 0))
    kern = functools.partial(_gla_sample_kernel, chunk=ts)
    return pl.pallas_call(
        kern,
        grid=(nseq,),
        in_specs=[row(GLA_K_WIDTH), row(GLA_K_WIDTH), row(GLA_V_WIDTH), row(GLA_K_WIDTH),
                  row(GLA_V_WIDTH), _const_spec((1, GLA_DV)), st],
        out_specs=[pl.BlockSpec((ts, GLA_V_WIDTH), lambda b: (b, 0)), st],
        out_shape=[jax.ShapeDtypeStruct((nseq * ts, GLA_V_WIDTH), BF16),
                   jax.ShapeDtypeStruct((nseq, GLA_K_WIDTH, GLA_DV), F32)],
        scratch_shapes=[pltpu.VMEM((GLA_K_WIDTH, GLA_DV), F32),
                        pltpu.VMEM((ts, GLA_K_WIDTH), F32),
                        pltpu.VMEM((ts, GLA_V_WIDTH), F32)],
        compiler_params=_params(("parallel",)),
        name="gla_sample",
    )(gq, gk, gv, la, gr, g, s0)


def _head_pair_perm():
    cols = []
    for p in range(ATTN_HEADS // ATTN_KV_HEADS):
        for g in range(ATTN_KV_HEADS):
            h = p + g * (ATTN_HEADS // ATTN_KV_HEADS)
            cols.extend(range(h * HEAD_DIM, (h + 1) * HEAD_DIM))
    return np.asarray(cols, np.int32)


def _prep_w_in(w):
    d = w.shape[0]
    splits = np.cumsum([ATTN_WIDTH, KV_WIDTH, KV_WIDTH, IDX_HEADS * IDX_DIM, IDX_DIM, IDX_HEADS,
                        GLA_K_WIDTH, GLA_K_WIDTH, GLA_V_WIDTH, GLA_V_WIDTH, GLA_GATE_RANK])[:-1]
    q, k, v, iq, ik, iw, gq, gk, gv, gr, glr = jnp.split(w, [int(s) for s in splits], axis=1)
    zeros = lambda c: jnp.zeros((d, c), w.dtype)
    parts = [q[:, _head_pair_perm()], k, v, iq, ik, ik, iw, zeros(LANES - IDX_HEADS),
             gq, gk, gv, gr, glr, zeros(LANES - GLA_GATE_RANK)]
    out = jnp.concatenate(parts, axis=1).astype(BF16)
    assert out.shape[1] == C_END
    return out


def kernel(x_prompt, x_sample, cache_k, cache_v, cache_idx_k, state_gla, page_table,
           attn_norm_g, w_in, idx_knorm_g, idx_knorm_b, gla_gate_up, gla_gate_b, gla_onorm_g,
           w_out, ffn_norm_g, w_gate, w_up, w_down, rel_bias, final_norm_g):
    B, T, D = x_prompt.shape
    DB, TS, _ = x_sample.shape
    depth = w_in.shape[0]
    n_pool, page = cache_k.shape[1], cache_k.shape[2]
    n_pages = page_table.shape[1]
    past_len = n_pages * page
    topk_p = min(IDX_TOPK_MAX, T // 4)
    topk_s = min(IDX_TOPK_MAX, (past_len + TS) // 4)
    NP, NS = B * T, DB * TS
    assert NP % TOKEN_TILE == 0 and NS % TOKEN_TILE == 0

    tq = min(ATTN_TILE, T)
    btab = _bias_tables(rel_bias, tq, tq, offsets=(tq, 0), key_major=True)
    cfar = btab[0, :, 0, tq - 1]
    ktot = past_len + page
    bias_s = _bias_tables(rel_bias, TS, ktot, offsets=(past_len,)).reshape(ATTN_HEADS * TS, ktot)

    ck = cache_k.transpose(0, 1, 3, 4, 2).reshape(depth, n_pool, KV_WIDTH, page)
    cv = cache_v.transpose(0, 1, 3, 4, 2).reshape(depth, n_pool, KV_WIDTH, page)
    cidx = cache_idx_k.transpose(0, 1, 3, 2)
    perm = _head_pair_perm()
    lane = np.arange(LANES)
    half_mask = np.stack([lane < HEAD_DIM, lane >= HEAD_DIM])

    xp = x_prompt.reshape(NP, D)
    xs = x_sample.reshape(NS, D)
    outs = {n: [] for n in ("kp", "vp", "ikp", "sp", "ks", "vs", "iks", "ss")}
    new_page = lambda a: jnp.pad(a.reshape(DB, TS, a.shape[-1]),
                                 ((0, 0), (0, page - TS), (0, 0))).transpose(0, 2, 1)
    for l in range(depth):
        w = _prep_w_in(w_in[l])
        lng = jnp.concatenate([idx_knorm_g[l], idx_knorm_g[l]])[None, :]
        lnb = jnp.concatenate([idx_knorm_b[l], idx_knorm_b[l]])[None, :]
        gup = jnp.zeros((LANES, GLA_K_WIDTH), BF16).at[:GLA_GATE_RANK].set(gla_gate_up[l].astype(BF16))
        proj = lambda x: _inproj(x, attn_norm_g[l][None, :], w, lng, lnb, gup,
                                 gla_gate_b[l][None, :])
        onorm = gla_onorm_g[l][None, :]
        wo = jnp.concatenate([w_out[l][:ATTN_WIDTH][perm], w_out[l][ATTN_WIDTH:]], axis=0).astype(BF16)
        ffn = lambda x, ao, go: _ffn(x, ao, go, wo, ffn_norm_g[l][None, :], w_gate[l].astype(BF16),
                                     w_up[l].astype(BF16), w_down[l].astype(BF16))

        (q, k32, v32, kb, vb, iq, ik32, ik2, iw, gq, gk, gv, gr, la) = proj(xp)
        ao = _dsa_prompt(iq, iw[:, :IDX_HEADS].T, q, ik2, kb, vb.T, btab, cfar, B, T, topk_p)
        go, s_p = _gla_prompt(gq, gk, gv, la, gr, onorm, B, T)
        xp = ffn(xp, ao, go)
        outs["kp"].append(k32.reshape(B, T, ATTN_KV_HEADS, HEAD_DIM))
        outs["vp"].append(v32.reshape(B, T, ATTN_KV_HEADS, HEAD_DIM))
        outs["ikp"].append(ik32.reshape(B, T, IDX_DIM))
        outs["sp"].append(s_p.reshape(B, GLA_HEADS, GLA_DK, GLA_DV))

        (q, k32, v32, kb, vb, iq, ik32, ik2, iw, gq, gk, gv, gr, la) = proj(xs)
        iq_s = iq.reshape(DB, TS, IDX_HEADS, IDX_DIM).transpose(0, 2, 1, 3)
        iq_s = iq_s.reshape(DB, IDX_HEADS * TS, IDX_DIM)
        iw_s = iw[:, :IDX_HEADS].reshape(DB, TS, IDX_HEADS).transpose(0, 2, 1)
        iw_s = jnp.broadcast_to(iw_s.reshape(DB, IDX_HEADS * TS, 1), (DB, IDX_HEADS * TS, LANES))
        q_s = q.reshape(DB, TS, ATTN_HEADS // 2, LANES)
        q_s = jnp.stack([jnp.where(half_mask[g], q_s, jnp.zeros_like(q_s))
                         for g in range(ATTN_KV_HEADS)], axis=1)
        q_s = q_s.transpose(0, 1, 3, 2, 4).reshape(DB, ATTN_HEADS * TS, LANES)
        ao = _dsa_sample(page_table, iq_s, iw_s, q_s, new_page(ik32.astype(BF16)), new_page(kb),
                         new_page(vb), bias_s, cidx, ck, cv, l, topk_s)
        ao = ao.reshape(DB, ATTN_HEADS // 2, TS, LANES).transpose(0, 2, 1, 3).reshape(NS, ATTN_WIDTH)
        s0 = state_gla[l].reshape(DB, GLA_K_WIDTH, GLA_DV)
        go, s_s = _gla_sample(gq, gk, gv, la, gr, onorm, s0, 0, DB, TS)
        xs = ffn(xs, ao, go)
        outs["ks"].append(k32.reshape(DB, TS, ATTN_KV_HEADS, HEAD_DIM))
        outs["vs"].append(v32.reshape(DB, TS, ATTN_KV_HEADS, HEAD_DIM))
        outs["iks"].append(ik32.reshape(DB, TS, IDX_DIM))
        outs["ss"].append(s_s.reshape(DB, GLA_HEADS, GLA_DK, GLA_DV))

    yp = _rmsnorm(xp, final_norm_g[None, :]).reshape(B, T, D)
    ys = _rmsnorm(xs, final_norm_g[None, :]).reshape(DB, TS, D)
    st = lambda n: jnp.stack(outs[n])
    return (yp, ys, st("kp"), st("vp"), st("ikp"), st("sp"), st("ks"), st("vs"), st("iks"), st("ss"))
```
